```python
import jax, jax.numpy as jnp
from jax import lax
import numpy as np

D_MODEL = 2048
BATCH = 2
SEQ = 4096
DEPTH = 4
DEC_BATCH = 8
DEC_SEQ = 32
PAST_LEN = 4096

CHUNK = 64
N_MIXERS = 2
N_MLSTM = (DEPTH + 1) // 2
N_FOX = DEPTH // 2
PLE_DIM = 256
D_FF = 5632
FFN_HALF = 0.5
NORM_EPS = 1e-6
M_HEADS = 8
M_DQK = D_MODEL // (2 * M_HEADS)
M_DV = D_MODEL // M_HEADS
M_QK = M_HEADS * M_DQK
M_V = M_HEADS * M_DV
M_IN = 2 * M_QK + 2 * M_V + 2 * M_HEADS
F_HEADS = 16
F_HD = D_MODEL // F_HEADS
F_IN = 4 * D_MODEL + F_HEADS
Q_BLOCK = 128

kernel_name = 'hybrid_mlstm_fox_streaming_step'


def rmsnorm(x, g):
    xf = x.astype(jnp.float32)
    y = xf * lax.rsqrt(jnp.mean(xf * xf, axis=-1, keepdims=True) + NORM_EPS)
    return (y * g.astype(jnp.float32)).astype(x.dtype)


def swiglu(x, w_in, w_out):
    g, u = jnp.split(x @ w_in, 2, axis=-1)
    return (jax.nn.silu(g) * u) @ w_out


def mlstm_recurrence(q, k, v, ig, lf, C0, n0, m0, block):
    B, T, H, _ = q.shape
    nc = T // block

    def to_blocks(a):
        a = a.reshape((B, nc, block) + a.shape[2:])
        return jnp.swapaxes(jnp.moveaxis(a, 1, 0), 2, 3)

    causal = jnp.tril(jnp.ones((block, block), bool))

    def step(carry, blk):
        C, n, m = carry
        qb, kb, vb, ib, fb = blk
        b = jnp.cumsum(fb, axis=-1)
        dmat = jnp.where(causal, b[..., :, None] - b[..., None, :] + ib[..., None, :], -jnp.inf)
        g = b + m[..., None]
        m_t = jnp.maximum(g, jnp.max(dmat, axis=-1))
        s = jnp.einsum('bhtd,bhsd->bhts', qb, kb) * jnp.exp(dmat - m_t[..., None])
        inter = jnp.exp(g - m_t)
        num = jnp.einsum('bhts,bhsv->bhtv', s, vb) + inter[..., None] * jnp.einsum('bhtd,bhdv->bhtv', qb, C)
        den = jnp.sum(s, axis=-1) + inter * jnp.einsum('bhtd,bhd->bht', qb, n)
        h = num / jnp.maximum(jnp.abs(den), jnp.exp(-m_t))[..., None]
        b_end = b[..., -1]
        m_new = m_t[..., -1]
        decay = jnp.exp(b_end + m - m_new)
        wa = jnp.exp(b_end[..., None] - b + ib - m_new[..., None])
        C_new = decay[..., None, None] * C + jnp.einsum('bhs,bhsd,bhsv->bhdv', wa, kb, vb)
        n_new = decay[..., None] * n + jnp.einsum('bhs,bhsd->bhd', wa, kb)
        return (C_new, n_new, m_new), h

    xs = (to_blocks(q), to_blocks(k), to_blocks(v), to_blocks(ig), to_blocks(lf))
    (C, n, m), h = lax.scan(step, (C0, n0, m0), xs)
    h = jnp.moveaxis(jnp.swapaxes(h, 2, 3), 0, 1).reshape(B, T, H, M_DV)
    return h, C, n, m


def mlstm_mixer(xn, w_in, b_gates, g_h, w_out, C0, n0, m0, block):
    B, T, _ = xn.shape
    f32 = jnp.float32
    q, k, v, o, ig, fg = jnp.split(xn @ w_in, [M_QK, 2 * M_QK, 2 * M_QK + M_V, 2 * M_QK + 2 * M_V,
                                              2 * M_QK + 2 * M_V + M_HEADS], axis=-1)
    q = q.reshape(B, T, M_HEADS, M_DQK).astype(f32)
    k = k.reshape(B, T, M_HEADS, M_DQK).astype(f32) * (M_DQK ** -0.5)
    v = v.reshape(B, T, M_HEADS, M_DV).astype(f32)
    ig = ig.astype(f32) + b_gates[0].astype(f32)
    lf = jax.nn.log_sigmoid(fg.astype(f32) + b_gates[1].astype(f32))
    h, C, n, m = mlstm_recurrence(q, k, v, ig, lf, C0.astype(f32), n0.astype(f32), m0.astype(f32), block)
    h = rmsnorm(h, g_h.reshape(M_HEADS, M_DV)).reshape(B, T, M_V).astype(xn.dtype)
    return (h * jax.nn.sigmoid(o)) @ w_out, (C, n, m)


def fox_project(xn, w_in, b_f, g_qk):
    B, T, _ = xn.shape
    D = D_MODEL
    q, k, v, og, fg = jnp.split(xn @ w_in, [D, 2 * D, 3 * D, 4 * D], axis=-1)
    q = rmsnorm(q.reshape(B, T, F_HEADS, F_HD), g_qk[0])
    k = rmsnorm(k.reshape(B, T, F_HEADS, F_HD), g_qk[1])
    v = v.reshape(B, T, F_HEADS, F_HD)
    lf = jax.nn.log_sigmoid(fg.astype(jnp.float32) + b_f.astype(jnp.float32))
    return q, k, v, og, lf


def fox_attend(q, k, v, c_q, c_k, pos_q, pos_k):
    logits = jnp.einsum('bqhd,bkhd->bhqk', q, k).astype(jnp.float32) * (F_HD ** -0.5)
    logits = logits + jnp.swapaxes(c_q, 1, 2)[..., :, None] - jnp.swapaxes(c_k, 1, 2)[..., None, :]
    logits = jnp.where(pos_k[None, :] <= pos_q[:, None], logits, -jnp.inf)
    p = jax.nn.softmax(logits, axis=-1)
    return jnp.einsum('bhqk,bkhd->bqhd', p.astype(v.dtype), v)


def fox_prompt(xn, w_in, b_f, g_qk, w_out):
    B, T, _ = xn.shape
    q, k, v, og, lf = fox_project(xn, w_in, b_f, g_qk)
    c = jnp.cumsum(lf, axis=1)
    pos = jnp.arange(T)
    nb = T // Q_BLOCK
    qs = jnp.moveaxis(q.reshape(B, nb, Q_BLOCK, F_HEADS, F_HD), 1, 0)
    cs = jnp.moveaxis(c.reshape(B, nb, Q_BLOCK, F_HEADS), 1, 0)
    ps = pos.reshape(nb, Q_BLOCK)
    o = lax.map(lambda a: fox_attend(a[0], k, v, a[1], c, a[2], pos), (qs, cs, ps))
    o = jnp.moveaxis(o, 0, 1).reshape(B, T, D_MODEL)
    return (o * jax.nn.sigmoid(og)) @ w_out, (k, v, lf)


def fox_sample(xn, k_past, v_past, lf_past, w_in, b_f, g_qk, w_out):
    B, T, _ = xn.shape
    P = k_past.shape[1]
    q, k, v, og, lf = fox_project(xn, w_in, b_f, g_qk)
    lf_past = lf_past.astype(jnp.float32)
    c_past = jnp.cumsum(lf_past, axis=1) - jnp.sum(lf_past, axis=1, keepdims=True)
    c_new = jnp.cumsum(lf, axis=1)
    keys = jnp.concatenate([k_past.astype(k.dtype), k], axis=1)
    vals = jnp.concatenate([v_past.astype(v.dtype), v], axis=1)
    c_k = jnp.concatenate([c_past, c_new], axis=1)
    o = fox_attend(q, keys, vals, c_new, c_k, P + jnp.arange(T), jnp.arange(P + T))
    o = o.reshape(B, T, D_MODEL)
    return (o * jax.nn.sigmoid(og)) @ w_out, (k, v, lf)


def run_trunk(x, p, prm, mlstm_init, fox_past):
    B, T, _ = x.shape
    block = min(CHUNK, T)
    m_states, f_rows = [], []
    for i in range(DEPTH):
        g = prm['norm_gains'][i]
        x = x + FFN_HALF * swiglu(rmsnorm(x, g[0]), prm['ffn1_in'][i], prm['ffn1_out'][i])
        xn = rmsnorm(x, g[1])
        j = i // N_MIXERS
        if i % N_MIXERS == 0:
            if mlstm_init is None:
                C0 = jnp.zeros((B, M_HEADS, M_DQK, M_DV), jnp.float32)
                n0 = jnp.zeros((B, M_HEADS, M_DQK), jnp.float32)
                m0 = jnp.zeros((B, M_HEADS), jnp.float32)
            else:
                C0, n0, m0 = mlstm_init[0][j], mlstm_init[1][j], mlstm_init[2][j]
            y, st = mlstm_mixer(xn, prm['mlstm_w_in'][j], prm['mlstm_b_gates'][j], prm['mlstm_g_h'][j],
                                prm['mlstm_w_out'][j], C0, n0, m0, block)
            m_states.append(st)
        else:
            if fox_past is None:
                y, rows = fox_prompt(xn, prm['fox_w_in'][j], prm['fox_b_f'][j], prm['fox_g_qk'][j],
                                     prm['fox_w_out'][j])
            else:
                y, rows = fox_sample(xn, fox_past[0][j], fox_past[1][j], fox_past[2][j],
                                     prm['fox_w_in'][j], prm['fox_b_f'][j], prm['fox_g_qk'][j],
                                     prm['fox_w_out'][j])
            f_rows.append(rows)
        x = x + y
        x = x + FFN_HALF * swiglu(rmsnorm(x, g[2]), prm['ffn2_in'][i], prm['ffn2_out'][i])
        gate = jax.nn.sigmoid(rmsnorm(x, g[3]) @ prm['ple_gate'][i])
        x = x + gate * (p[i].astype(x.dtype) @ prm['ple_proj'][i])
    C = jnp.stack([s[0] for s in m_states])
    n = jnp.stack([s[1] for s in m_states])
    m = jnp.stack([s[2] for s in m_states])
    k = jnp.stack([r[0] for r in f_rows])
    v = jnp.stack([r[1] for r in f_rows])
    lf = jnp.stack([r[2] for r in f_rows])
    return x, C, n, m, k, v, lf


def setup_inputs(seed: int = 0) -> dict:
    key = jax.random.key(seed)
    ks = jax.random.split(key, 32)
    f32 = jnp.float32
    D = D_MODEL

    def nrm(k, shape, scale):
        return jax.random.normal(k, shape, f32) * scale

    return {
        'x_prompt': nrm(ks[0], (BATCH, SEQ, D), 1.0),
        'x_sample': nrm(ks[1], (DEC_BATCH, DEC_SEQ, D), 1.0),
        'p_prompt': nrm(ks[2], (DEPTH, BATCH, SEQ, PLE_DIM), 1.0),
        'p_sample': nrm(ks[3], (DEPTH, DEC_BATCH, DEC_SEQ, PLE_DIM), 1.0),
        'state_mlstm_C': nrm(ks[4], (N_MLSTM, DEC_BATCH, M_HEADS, M_DQK, M_DV), 0.5),
        'state_mlstm_n': nrm(ks[5], (N_MLSTM, DEC_BATCH, M_HEADS, M_DQK), 0.5),
        'state_mlstm_m': nrm(ks[6], (N_MLSTM, DEC_BATCH, M_HEADS), 1.0),
        'cache_fox_k': nrm(ks[7], (N_FOX, DEC_BATCH, PAST_LEN, F_HEADS, F_HD), 1.0),
        'cache_fox_v': nrm(ks[8], (N_FOX, DEC_BATCH, PAST_LEN, F_HEADS, F_HD), 1.0),
        'cache_fox_lf': jax.nn.log_sigmoid(3.0 + jax.random.normal(ks[9], (N_FOX, DEC_BATCH, PAST_LEN, F_HEADS), f32)),
        'norm_gains': 1.0 + nrm(ks[10], (DEPTH, 4, D), 0.1),
        'ffn1_in': nrm(ks[11], (DEPTH, D, 2 * D_FF), D ** -0.5),
        'ffn1_out': nrm(ks[12], (DEPTH, D_FF, D), D_FF ** -0.5),
        'ffn2_in': nrm(ks[13], (DEPTH, D, 2 * D_FF), D ** -0.5),
        'ffn2_out': nrm(ks[14], (DEPTH, D_FF, D), D_FF ** -0.5),
        'ple_gate': nrm(ks[15], (DEPTH, D, D), D ** -0.5),
        'ple_proj': nrm(ks[16], (DEPTH, PLE_DIM, D), PLE_DIM ** -0.5),
        'mlstm_w_in': nrm(ks[17], (N_MLSTM, D, M_IN), D ** -0.5),
        'mlstm_b_gates': jnp.stack([nrm(ks[18], (N_MLSTM, M_HEADS), 0.1),
                                    jax.random.uniform(ks[19], (N_MLSTM, M_HEADS), f32, 3.0, 6.0)], axis=1),
        'mlstm_g_h': 1.0 + nrm(ks[20], (N_MLSTM, M_V), 0.1),
        'mlstm_w_out': nrm(ks[21], (N_MLSTM, M_V, D), M_V ** -0.5),
        'fox_w_in': nrm(ks[22], (N_FOX, D, F_IN), D ** -0.5),
        'fox_b_f': jax.random.uniform(ks[23], (N_FOX, F_HEADS), f32, 2.0, 5.0),
        'fox_g_qk': 1.0 + nrm(ks[24], (N_FOX, 2, F_HD), 0.1),
        'fox_w_out': nrm(ks[25], (N_FOX, D, D), D ** -0.5),
    }


def reference(x_prompt, x_sample, p_prompt, p_sample, state_mlstm_C, state_mlstm_n, state_mlstm_m,
              cache_fox_k, cache_fox_v, cache_fox_lf, norm_gains, ffn1_in, ffn1_out, ffn2_in, ffn2_out,
              ple_gate, ple_proj, mlstm_w_in, mlstm_b_gates, mlstm_g_h, mlstm_w_out,
              fox_w_in, fox_b_f, fox_g_qk, fox_w_out):
    prm = {
        'norm_gains': norm_gains, 'ffn1_in': ffn1_in, 'ffn1_out': ffn1_out,
        'ffn2_in': ffn2_in, 'ffn2_out': ffn2_out, 'ple_gate': ple_gate, 'ple_proj': ple_proj,
        'mlstm_w_in': mlstm_w_in, 'mlstm_b_gates': mlstm_b_gates, 'mlstm_g_h': mlstm_g_h,
        'mlstm_w_out': mlstm_w_out, 'fox_w_in': fox_w_in, 'fox_b_f': fox_b_f,
        'fox_g_qk': fox_g_qk, 'fox_w_out': fox_w_out,
    }
    y_prompt, pC, pn, pm, pk, pv, plf = run_trunk(x_prompt, p_prompt, prm, None, None)
    y_sample, sC, sn, sm, sk, sv, slf = run_trunk(
        x_sample, p_sample, prm,
        (state_mlstm_C, state_mlstm_n, state_mlstm_m),
        (cache_fox_k, cache_fox_v, cache_fox_lf))
    return (y_prompt, y_sample, pC, pn, pm, pk, pv, plf, sC, sn, sm, sk, sv, slf)
```

```python
import functools

import jax
import jax.numpy as jnp
from jax import lax
from jax.experimental import pallas as pl
from jax.experimental.pallas import tpu as pltpu

F32 = jnp.float32
BF16 = jnp.bfloat16
NORM_EPS = 1e-6
FFN_HALF = 0.5
V7X_VMEM_BYTES = 64 * 1024 * 1024
VMEM_LIMIT = V7X_VMEM_BYTES - 8 * 1024 * 1024
LANES = 128
SUBLANES = 8
NEG_INF = float("-inf")


def _params(*sem):
    return pltpu.CompilerParams(dimension_semantics=sem, vmem_limit_bytes=VMEM_LIMIT)


def _tile(n, pref):
    if n <= pref:
        return n
    for t in range(pref - pref % LANES, 0, -LANES):
        if n % t == 0:
            return t
    raise ValueError((n, pref))


def _rms(x, g):
    ms = jnp.mean(x * x, axis=-1, keepdims=True)
    return x * lax.rsqrt(ms + NORM_EPS) * g


def _sigmoid(x):
    return 1.0 / (1.0 + jnp.exp(-x))


def _log_sigmoid(x):
    return jnp.minimum(x, 0.0) - jnp.log(1.0 + jnp.exp(-jnp.abs(x)))


def _dot(a, b):
    return jnp.dot(a, b, preferred_element_type=F32)


def _dot_nt(a, b):
    return lax.dot_general(a, b, (((1,), (1,)), ((), ())), preferred_element_type=F32)


def _dot_tn(a, b):
    return lax.dot_general(a, b, (((0,), (0,)), ((), ())), preferred_element_type=F32)


def _dot_f32(a, b):
    return jnp.dot(a, b, preferred_element_type=F32, precision=lax.Precision.HIGHEST)


def _ffn_body(x_ref, g_ref, wg_ref, wu_ref, wo_ref, out_ref, xn_ref, *, col_chunk):
    f = pl.program_id(1)

    @pl.when(f == 0)
    def _():
        x = x_ref[...]
        xn_ref[...] = _rms(x, g_ref[...]).astype(BF16)
        out_ref[...] = x

    xn = xn_ref[...]
    g = _dot(xn, wg_ref[...])
    u = _dot(xn, wu_ref[...])
    h = (g * _sigmoid(g) * u * FFN_HALF).astype(BF16)
    d = out_ref.shape[1]
    for c in range(d // col_chunk):
        sl = slice(c * col_chunk, (c + 1) * col_chunk)
        out_ref[:, sl] += _dot(h, wo_ref[:, sl])


def _ffn(x, gain, w_in, w_out, layer):
    m, d = x.shape
    f_dim = w_out.shape[1]
    tm = _tile(m, 1024)
    tf = _tile(f_dim, 512)
    nf = f_dim // tf
    return pl.pallas_call(
        functools.partial(_ffn_body, col_chunk=_tile(d, 512)),
        out_shape=jax.ShapeDtypeStruct((m, d), F32),
        grid=(m // tm, nf),
        in_specs=[
            pl.BlockSpec((tm, d), lambda i, f: (i, 0), pipeline_mode=pl.Buffered(1)),
            pl.BlockSpec((1, d), lambda i, f: (0, 0)),
            pl.BlockSpec((None, d, tf), lambda i, f: (layer, 0, f)),
            pl.BlockSpec((None, d, tf), lambda i, f: (layer, 0, f + nf)),
            pl.BlockSpec((None, tf, d), lambda i, f: (layer, f, 0)),
        ],
        out_specs=pl.BlockSpec((tm, d), lambda i, f: (i, 0)),
        scratch_shapes=[pltpu.VMEM((tm, d), BF16)],
        compiler_params=_params("parallel", "arbitrary"),
        name="ffn",
    )(x, gain, w_in, w_in, w_out)


def _proj_body(*refs, mode, hd, scale):
    x_ref, g_ref, w_ref = refs[:3]
    xn_ref = refs[-1]
    j = pl.program_id(1)

    @pl.when(j == 0)
    def _():
        xn_ref[...] = _rms(x_ref[...], g_ref[...]).astype(BF16)

    z = _dot(xn_ref[...], w_ref[...])
    if mode == "plain":
        refs[3][...] = z
    elif mode == "dual":
        refs[3][...] = z
        refs[4][...] = z.astype(BF16)
    else:
        gh_ref = refs[3]
        outs = refs[4:-1]
        for c in range(z.shape[1] // hd):
            sl = slice(c * hd, (c + 1) * hd)
            y = _rms(z[:, sl], gh_ref[...])
            if mode == "headnorm_q":
                outs[0][:, sl] = (y * scale).astype(BF16)
            else:
                outs[0][:, sl] = y
                outs[1][:, sl] = y.astype(BF16)


def _proj(x, gain, w, layer, col0, ncols, mode, head_gain=None, hd=None, scale=None):
    m, d = x.shape
    tm = _tile(m, 1024)
    tn = _tile(ncols, 1024)
    assert col0 % tn == 0
    jb = col0 // tn
    in_specs = [
        pl.BlockSpec((tm, d), lambda i, j: (i, 0), pipeline_mode=pl.Buffered(1)),
        pl.BlockSpec((1, d), lambda i, j: (0, 0)),
        pl.BlockSpec((None, d, tn), lambda i, j: (layer, 0, jb + j)),
    ]
    args = [x, gain, w]
    o_spec = pl.BlockSpec((tm, tn), lambda i, j: (i, j))
    if mode == "plain":
        out_shape = jax.ShapeDtypeStruct((m, ncols), F32)
        out_specs = o_spec
    elif mode == "headnorm_q":
        out_shape = jax.ShapeDtypeStruct((m, ncols), BF16)
        out_specs = o_spec
    else:
        out_shape = (jax.ShapeDtypeStruct((m, ncols), F32), jax.ShapeDtypeStruct((m, ncols), BF16))
        out_specs = (o_spec, o_spec)
    if mode.startswith("headnorm"):
        in_specs.append(pl.BlockSpec((1, hd), lambda i, j: (0, 0)))
        args.append(head_gain)
    return pl.pallas_call(
        functools.partial(_proj_body, mode=mode, hd=hd, scale=scale),
        out_shape=out_shape,
        grid=(m // tm, ncols // tn),
        in_specs=in_specs,
        out_specs=out_specs,
        scratch_shapes=[pltpu.VMEM((tm, d), BF16)],
        compiler_params=_params("parallel", "arbitrary"),
        name="proj_" + mode,
    )(*args)


def _gate_body(x_ref, g_ref, wt_ref, o_ref):
    xn = _rms(x_ref[...], g_ref[...]).astype(BF16)
    o_ref[...] = _dot_nt(wt_ref[...], xn)


def _gate_proj(x, gain, wt):
    m, d = x.shape
    ng = wt.shape[0]
    tm = _tile(m, 1024)
    return pl.pallas_call(
        _gate_body,
        out_shape=jax.ShapeDtypeStruct((ng, m), F32),
        grid=(m // tm,),
        in_specs=[
            pl.BlockSpec((tm, d), lambda i: (i, 0)),
            pl.BlockSpec((1, d), lambda i: (0, 0)),
            pl.BlockSpec((ng, d), lambda i: (0, 0)),
        ],
        out_specs=pl.BlockSpec((ng, tm), lambda i: (0, i)),
        compiler_params=_params("parallel"),
        name="gate_proj",
    )(x, gain, wt)


def _out_body(a_ref, w_ref, x_ref, o_ref):
    o_ref[...] = x_ref[...] + _dot(a_ref[...], w_ref[...])


def _out_proj(a, w, x, layer):
    m, k = a.shape
    d = x.shape[1]
    tm = _tile(m, 1024)
    tn = _tile(d, 1024)
    return pl.pallas_call(
        _out_body,
        out_shape=jax.ShapeDtypeStruct((m, d), F32),
        grid=(m // tm, d // tn),
        in_specs=[
            pl.BlockSpec((tm, k), lambda i, j: (i, 0)),
            pl.BlockSpec((None, k, tn), lambda i, j: (layer, 0, j)),
            pl.BlockSpec((tm, tn), lambda i, j: (i, j)),
        ],
        out_specs=pl.BlockSpec((tm, tn), lambda i, j: (i, j)),
        compiler_params=_params("parallel", "arbitrary"),
        name="out_proj",
    )(a, w, x)


def _ple_body(x_ref, g_ref, wg_ref, p_ref, wp_ref, xr_ref, o_ref, xn_ref):
    j = pl.program_id(1)

    @pl.when(j == 0)
    def _():
        xn_ref[...] = _rms(x_ref[...], g_ref[...]).astype(BF16)

    gate = _sigmoid(_dot(xn_ref[...], wg_ref[...]))
    emb = _dot(p_ref[...].astype(BF16), wp_ref[...])
    o_ref[...] = xr_ref[...] + gate * emb


def _ple(x, gain, w_gate, p, w_proj, layer):
    m, d = x.shape
    pd = p.shape[-1]
    tm = _tile(m, 1024)
    tn = _tile(d, 1024)
    return pl.pallas_call(
        _ple_body,
        out_shape=jax.ShapeDtypeStruct((m, d), F32),
        grid=(m // tm, d // tn),
        in_specs=[
            pl.BlockSpec((tm, d), lambda i, j: (i, 0), pipeline_mode=pl.Buffered(1)),
            pl.BlockSpec((1, d), lambda i, j: (0, 0)),
            pl.BlockSpec((None, d, tn), lambda i, j: (layer, 0, j)),
            pl.BlockSpec((None, tm, pd), lambda i, j: (layer, i, 0)),
            pl.BlockSpec((None, pd, tn), lambda i, j: (layer, 0, j)),
            pl.BlockSpec((tm, tn), lambda i, j: (i, j)),
        ],
        out_specs=pl.BlockSpec((tm, tn), lambda i, j: (i, j)),
        scratch_shapes=[pltpu.VMEM((tm, d), BF16)],
        compiler_params=_params("parallel", "arbitrary"),
        name="ple",
    )(x, gain, w_gate, p, w_proj, x)


def _mlstm_body(q_ref, k_ref, v_ref, o_ref, gc_ref, gr_ref, bc_ref, br_ref, gh_ref,
                c0_ref, n0_ref, m0_ref, a_ref, c_ref, n_ref, m_ref, *, heads, dqk, dv):
    ci = pl.program_id(1)
    L = q_ref.shape[0]

    @pl.when(ci == 0)
    def _():
        c_ref[...] = c0_ref[...]
        n_ref[...] = n0_ref[...]
        m_ref[...] = m0_ref[...]

    row = lax.broadcasted_iota(jnp.int32, (L, L), 0)
    col = lax.broadcasted_iota(jnp.int32, (L, L), 1)
    causal = row >= col
    tril = causal.astype(F32)
    triu = (row <= col).astype(F32)

    gc = gc_ref[...] + bc_ref[...]
    gr = gr_ref[...] + br_ref[...]
    b_col = _dot_f32(tril, _log_sigmoid(gc))
    b_row = _dot_f32(_log_sigmoid(gr), triu)
    scale = dqk ** -0.5

    for h in range(heads):
        q = q_ref[:, h * dqk:(h + 1) * dqk]
        k = k_ref[:, h * dqk:(h + 1) * dqk] * scale
        v = v_ref[:, h * dv:(h + 1) * dv].astype(BF16)
        qb = q.astype(BF16)
        cst = c_ref[h]
        nst = n_ref[h]
        m_prev = m_ref[h][:, :1]

        bc = b_col[:, heads + h:heads + h + 1]
        ic = gc[:, h:h + 1]
        brow = b_row[heads + h:heads + h + 1, :]
        irow = gr[h:h + 1, :]

        dmat = jnp.where(causal, bc - brow + irow, NEG_INF)
        g = bc + m_prev
        m_t = jnp.maximum(g, jnp.max(dmat, axis=-1, keepdims=True))
        s = _dot_nt(qb, k.astype(BF16)) * jnp.exp(dmat - m_t)
        inter = jnp.exp(g - m_t)
        num = _dot(s.astype(BF16), v) + inter * _dot(qb, cst.astype(BF16))
        den = jnp.sum(s, axis=-1, keepdims=True) + inter * jnp.sum(q * nst, axis=-1, keepdims=True)
        hval = num / jnp.maximum(jnp.abs(den), jnp.exp(-m_t))

        hn = _rms(hval, gh_ref[:, h * dv:(h + 1) * dv])
        og = o_ref[:, h * dv:(h + 1) * dv]
        a_ref[:, h * dv:(h + 1) * dv] = (hn * _sigmoid(og)).astype(BF16)

        b_end = bc[L - 1:L, :]
        m_new = m_t[L - 1:L, :]
        decay = jnp.exp(b_end + m_prev - m_new)
        wa = jnp.exp(b_end - bc + ic - m_new)
        kw = k * wa
        c_ref[h] = decay * cst + _dot_tn(kw.astype(BF16), v)
        n_ref[h] = decay * nst + jnp.sum(kw, axis=0, keepdims=True)
        m_ref[h] = jnp.broadcast_to(m_new, (1, LANES))


def _mlstm(z, gates_col, gates_row, bias_lanes, bias_rows, g_h, c0, n0, m0, chunk):
    b, t, _ = z.shape
    heads, dqk, dv = c0.shape[1:]
    qk = heads * dqk
    vd = heads * dv
    assert (2 * qk) % vd == 0
    v_blk = 2 * qk // vd
    ng = gates_row.shape[1]
    nc = t // chunk
    st4 = lambda i, c: (i, 0, 0, 0)
    return pl.pallas_call(
        functools.partial(_mlstm_body, heads=heads, dqk=dqk, dv=dv),
        out_shape=(
            jax.ShapeDtypeStruct((b, t, vd), BF16),
            jax.ShapeDtypeStruct((b, heads, dqk, dv), F32),
            jax.ShapeDtypeStruct((b, heads, 1, dqk), F32),
            jax.ShapeDtypeStruct((b, heads, 1, LANES), F32),
        ),
        grid=(b, nc),
        in_specs=[
            pl.BlockSpec((None, chunk, qk), lambda i, c: (i, c, 0)),
            pl.BlockSpec((None, chunk, qk), lambda i, c: (i, c, 1)),
            pl.BlockSpec((None, chunk, vd), lambda i, c: (i, c, v_blk)),
            pl.BlockSpec((None, chunk, vd), lambda i, c: (i, c, v_blk + 1)),
            pl.BlockSpec((None, chunk, LANES), lambda i, c: (i, c, 0)),
            pl.BlockSpec((None, ng, chunk), lambda i, c: (i, 0, c)),
            pl.BlockSpec((1, LANES), lambda i, c: (0, 0)),
            pl.BlockSpec((ng, 1), lambda i, c: (0, 0)),
            pl.BlockSpec((1, vd), lambda i, c: (0, 0)),
            pl.BlockSpec((None, heads, dqk, dv), st4),
            pl.BlockSpec((None, heads, 1, dqk), st4),
            pl.BlockSpec((None, heads, 1, LANES), st4),
        ],
        out_specs=(
            pl.BlockSpec((None, chunk, vd), lambda i, c: (i, c, 0)),
            pl.BlockSpec((None, heads, dqk, dv), st4),
            pl.BlockSpec((None, heads, 1, dqk), st4),
            pl.BlockSpec((None, heads, 1, LANES), st4),
        ),
        compiler_params=_params("parallel", "arbitrary"),
        name="mlstm",
    )(z, z, z, z, gates_col, gates_row, bias_lanes, bias_rows, g_h, c0, n0, m0)


def _cumsum_body(f_ref, b_ref, lf_ref, c_ref, *, blk, activate, anchor_end):
    t = f_ref.shape[1]
    row = lax.broadcasted_iota(jnp.int32, (blk, blk), 0)
    col = lax.broadcasted_iota(jnp.int32, (blk, blk), 1)
    triu = (row <= col).astype(F32)
    carry = jnp.zeros((f_ref.shape[0], 1), F32)
    for i in range(t // blk):
        sl = slice(i * blk, (i + 1) * blk)
        seg = f_ref[:, sl]
        if activate:
            seg = _log_sigmoid(seg + b_ref[...])
        lf_ref[:, sl] = seg
        cs = _dot_f32(seg, triu) + carry
        c_ref[:, sl] = cs
        carry = cs[:, blk - 1:blk]
    if anchor_end:
        c_ref[...] = c_ref[...] - carry


def _forget_cumsum(f_rows, bias_col, activate, anchor_end):
    b, ng, t = f_rows.shape
    blk = _tile(t, 512)
    spec = pl.BlockSpec((None, ng, t), lambda i: (i, 0, 0))
    return pl.pallas_call(
        functools.partial(_cumsum_body, blk=blk, activate=activate, anchor_end=anchor_end),
        out_shape=(jax.ShapeDtypeStruct((b, ng, t), F32), jax.ShapeDtypeStruct((b, ng, t), F32)),
        grid=(b,),
        in_specs=[spec, pl.BlockSpec((ng, 1), lambda i: (0, 0))],
        out_specs=(spec, spec),
        compiler_params=_params("parallel"),
        name="forget_cumsum",
    )(f_rows, bias_col)


def _online_softmax_step(s, v, m, l, acc):
    m_new = jnp.maximum(m, jnp.max(s, axis=-1, keepdims=True))
    p = jnp.exp(s - m_new)
    alpha = jnp.exp(m - m_new)
    l = alpha * l + jnp.sum(p, axis=-1, keepdims=True)
    acc = alpha * acc + _dot(p.astype(BF16), v)
    return m_new, l, acc


def _fox_prompt_body(q_ref, k_ref, v_ref, og_ref, cc_ref, cr_ref, a_ref, cq_ref, *, tq):
    h = pl.program_id(1)
    t = q_ref.shape[0]
    lane = lax.broadcasted_iota(jnp.int32, cc_ref.shape, 1)
    cq_ref[...] = jnp.sum(jnp.where(lane == h, cc_ref[...], 0.0), axis=-1, keepdims=True)
    row = lax.broadcasted_iota(jnp.int32, (tq, tq), 0)
    col = lax.broadcasted_iota(jnp.int32, (tq, tq), 1)
    causal = row >= col

    for qi in range(t // tq):
        q = q_ref[qi * tq:(qi + 1) * tq, :]
        cq = cq_ref[qi * tq:(qi + 1) * tq, :]

        def scores(kj):
            off = kj * tq if isinstance(kj, int) else pl.multiple_of(kj * tq, tq)
            k = k_ref[pl.ds(off, tq), :]
            v = v_ref[pl.ds(off, tq), :]
            ck = cr_ref[pl.ds(kj, 1), :]
            return _dot_nt(q, k) + cq - ck, v

        def body(kj, carry):
            s, v = scores(kj)
            return _online_softmax_step(s, v, *carry)

        init = (jnp.full((tq, 1), NEG_INF, F32), jnp.zeros((tq, 1), F32),
                jnp.zeros((tq, q.shape[1]), F32))
        m, l, acc = lax.fori_loop(0, qi, body, init)
        s, v = scores(qi)
        m, l, acc = _online_softmax_step(jnp.where(causal, s, NEG_INF), v, m, l, acc)
        og = og_ref[qi * tq:(qi + 1) * tq, :]
        a_ref[qi * tq:(qi + 1) * tq, :] = (acc / l * _sigmoid(og)).astype(BF16)


def _fox_prompt_attn(qb, kb, vb, og, c_col, c_row, heads):
    b, t, d = qb.shape
    hd = d // heads
    tq = _tile(t, 512)
    nk = t // tq
    c_row4 = c_row.reshape(b, c_row.shape[1], nk, tq)
    head_blk = pl.BlockSpec((None, t, hd), lambda i, h: (i, 0, h))
    return pl.pallas_call(
        functools.partial(_fox_prompt_body, tq=tq),
        out_shape=jax.ShapeDtypeStruct((b, t, d), BF16),
        grid=(b, heads),
        in_specs=[
            head_blk, head_blk, head_blk, head_blk,
            pl.BlockSpec((None, t, LANES), lambda i, h: (i, 0, 0)),
            pl.BlockSpec((None, None, nk, tq), lambda i, h: (i, h, 0, 0)),
        ],
        out_specs=head_blk,
        scratch_shapes=[pltpu.VMEM((t, 1), F32)],
        compiler_params=_params("parallel", "arbitrary"),
        name="fox_prompt_attn",
    )(qb, kb, vb, og, c_col, c_row4)


def _fox_sample_body(q_ref, kp_ref, vp_ref, kn_ref, vn_ref, og_ref, cq_ref, ckp_ref, ckn_ref,
                     a_ref, m_ref, l_ref, acc_ref, *, heads, hd):
    j = pl.program_id(1)
    nj = pl.num_programs(1)
    tq = q_ref.shape[0]

    @pl.when(j == 0)
    def _():
        m_ref[...] = jnp.full(m_ref.shape, NEG_INF, F32)
        l_ref[...] = jnp.zeros(l_ref.shape, F32)
        acc_ref[...] = jnp.zeros(acc_ref.shape, F32)

    def update(h, s, v):
        m, l, acc = _online_softmax_step(s, v, m_ref[h], l_ref[h], acc_ref[h])
        m_ref[h] = m
        l_ref[h] = l
        acc_ref[h] = acc

    for h in range(heads):
        sl = slice(h * hd, (h + 1) * hd)
        s = _dot_nt(q_ref[:, sl], kp_ref[:, sl].astype(BF16))
        s = s + cq_ref[:, h:h + 1] - ckp_ref[h:h + 1, :]
        update(h, s, vp_ref[:, sl].astype(BF16))

    @pl.when(j == nj - 1)
    def _():
        row = lax.broadcasted_iota(jnp.int32, (tq, tq), 0)
        col = lax.broadcasted_iota(jnp.int32, (tq, tq), 1)
        causal = row >= col
        for h in range(heads):
            sl = slice(h * hd, (h + 1) * hd)
            s = _dot_nt(q_ref[:, sl], kn_ref[:, sl])
            s = s + cq_ref[:, h:h + 1] - ckn_ref[h:h + 1, :]
            update(h, jnp.where(causal, s, NEG_INF), vn_ref[:, sl])
            a_ref[:, sl] = (acc_ref[h] / l_ref[h] * _sigmoid(og_ref[:, sl])).astype(BF16)


def _fox_sample_attn(qb, k_past, v_past, layer, kb, vb, og, cq_col, ck_past_row, ck_new_row, heads):
    b, tq, d = qb.shape
    hd = d // heads
    p = k_past.shape[2]
    ng = ck_past_row.shape[1]
    tk = _tile(p, 512)
    new_blk = pl.BlockSpec((None, tq, d), lambda i, j: (i, 0, 0))
    past_blk = pl.BlockSpec((None, None, tk, d), lambda i, j: (layer, i, j, 0))
    return pl.pallas_call(
        functools.partial(_fox_sample_body, heads=heads, hd=hd),
        out_shape=jax.ShapeDtypeStruct((b, tq, d), BF16),
        grid=(b, p // tk),
        in_specs=[
            new_blk, past_blk, past_blk, new_blk, new_blk, new_blk,
            pl.BlockSpec((None, tq, LANES), lambda i, j: (i, 0, 0)),
            pl.BlockSpec((None, ng, tk), lambda i, j: (i, 0, j)),
            pl.BlockSpec((None, ng, tq), lambda i, j: (i, 0, 0)),
        ],
        out_specs=new_blk,
        scratch_shapes=[
            pltpu.VMEM((heads, tq, 1), F32),
            pltpu.VMEM((heads, tq, 1), F32),
            pltpu.VMEM((heads, tq, hd), F32),
        ],
        compiler_params=_params("parallel", "arbitrary"),
        name="fox_sample_attn",
    )(qb, k_past, v_past, kb, vb, og, cq_col, ck_past_row, ck_new_row)


def _round_up(n, k):
    return (n + k - 1) // k * k


def _gate_weight_rows(w, col0, n):
    wt = jnp.transpose(w[:, col0:col0 + n]).astype(BF16)
    return jnp.pad(wt, ((0, _round_up(n, SUBLANES) - n), (0, 0)))


def _pad_rows(a, rows):
    return jnp.pad(a, ((0, rows - a.shape[0]),) + ((0, 0),) * (a.ndim - 1))


def _col_layout(rows):
    c = jnp.swapaxes(rows, 1, 2)
    return jnp.pad(c, ((0, 0), (0, 0), (0, LANES - c.shape[-1])))


def _run_trunk(x, p, w, mlstm_init, fox_past):
    b, t, d = x.shape
    depth = w["norm_gains"].shape[0]
    m_heads, dqk, dv = w["m_heads"], w["dqk"], w["dv"]
    f_heads = w["f_heads"]
    hd = d // f_heads
    qk, vd = m_heads * dqk, m_heads * dv
    mtok = b * t
    xf = x.reshape(mtok, d)
    pf = p.reshape(depth, mtok, p.shape[-1])
    chunk = _tile(t, 256)
    m_states, f_rows = [], []
    for i in range(depth):
        gains = w["norm_gains"][i]
        gain = lambda r: gains[r].reshape(1, d)
        j = i // 2
        xf = _ffn(xf, gain(0), w["ffn1_in"], w["ffn1_out"], i)
        if i % 2 == 0:
            z = _proj(xf, gain(1), w["mlstm_w_in"], j, 0, 2 * qk + 2 * vd, "plain")
            g_rows = _gate_proj(xf, gain(1), w["mlstm_gate_rows"][j])
            ng = g_rows.shape[0]
            g_rows = jnp.swapaxes(g_rows.reshape(ng, b, t), 0, 1)
            if mlstm_init is None:
                c0 = jnp.zeros((b, m_heads, dqk, dv), F32)
                n0 = jnp.zeros((b, m_heads, dqk), F32)
                m0 = jnp.zeros((b, m_heads), F32)
            else:
                c0, n0, m0 = mlstm_init[0][j], mlstm_init[1][j], mlstm_init[2][j]
            bias = w["mlstm_b_gates"][j].reshape(2 * m_heads)
            a, c_new, n_new, m_new = _mlstm(
                z.reshape(b, t, -1), _col_layout(g_rows), g_rows,
                jnp.pad(bias, (0, LANES - 2 * m_heads)).reshape(1, LANES),
                _pad_rows(bias.reshape(-1, 1), ng),
                w["mlstm_g_h"][j].reshape(1, vd),
                c0, n0.reshape(b, m_heads, 1, dqk),
                jnp.broadcast_to(m0[:, :, None, None], (b, m_heads, 1, LANES)), chunk)
            m_states.append((c_new, n_new.reshape(b, m_heads, dqk), m_new[:, :, 0, 0]))
            xf = _out_proj(a.reshape(mtok, vd), w["mlstm_w_out"], xf, j)
        else:
            g_qk = w["fox_g_qk"][j]
            qb = _proj(xf, gain(1), w["fox_w_in"], j, 0, d, "headnorm_q",
                       head_gain=g_qk[0].reshape(1, hd), hd=hd, scale=hd ** -0.5)
            kf, kb = _proj(xf, gain(1), w["fox_w_in"], j, d, d, "headnorm_kv",
                           head_gain=g_qk[1].reshape(1, hd), hd=hd)
            vf, vb = _proj(xf, gain(1), w["fox_w_in"], j, 2 * d, d, "dual")
            og = _proj(xf, gain(1), w["fox_w_in"], j, 3 * d, d, "plain")
            f_pre = _gate_proj(xf, gain(1), w["fox_gate_rows"][j])
            ng = f_pre.shape[0]
            f_pre = jnp.swapaxes(f_pre.reshape(ng, b, t), 0, 1)
            bias_col = _pad_rows(w["fox_b_f"][j].reshape(-1, 1), ng)
            lf_row, c_row = _forget_cumsum(f_pre, bias_col, True, False)
            shp = (b, t, d)
            if fox_past is None:
                a = _fox_prompt_attn(qb.reshape(shp), kb.reshape(shp), vb.reshape(shp),
                                     og.reshape(shp), _col_layout(c_row), c_row, f_heads)
            else:
                lf_past_row = jnp.swapaxes(fox_past[2][j].astype(F32), 1, 2)
                lf_past_row = jnp.pad(lf_past_row, ((0, 0), (0, ng - f_heads), (0, 0)))
                _, c_past_row = _forget_cumsum(lf_past_row, bias_col, False, True)
                a = _fox_sample_attn(qb.reshape(shp), fox_past[0], fox_past[1], j,
                                     kb.reshape(shp), vb.reshape(shp), og.reshape(shp),
                                     _col_layout(c_row), c_past_row, c_row, f_heads)
            lf = jnp.swapaxes(lf_row[:, :f_heads, :], 1, 2)
            f_rows.append((kf.reshape(b, t, f_heads, hd), vf.reshape(b, t, f_heads, hd), lf))
            xf = _out_proj(a.reshape(mtok, d), w["fox_w_out"], xf, j)
        xf = _ffn(xf, gain(2), w["ffn2_in"], w["ffn2_out"], i)
        xf = _ple(xf, gain(3), w["ple_gate"], pf, w["ple_proj"], i)
    stack = lambda k: jnp.stack([s[k] for s in m_states])
    stack_f = lambda k: jnp.stack([r[k] for r in f_rows])
    return (xf.reshape(b, t, d), stack(0), stack(1), stack(2), stack_f(0), stack_f(1), stack_f(2))


def kernel(x_prompt, x_sample, p_prompt, p_sample, state_mlstm_C, state_mlstm_n, state_mlstm_m, cache_fox_k, cache_fox_v, cache_fox_lf, norm_gains, ffn1_in, ffn1_out, ffn2_in, ffn2_out, ple_gate, ple_proj, mlstm_w_in, mlstm_b_gates, mlstm_g_h, mlstm_w_out, fox_w_in, fox_b_f, fox_g_qk, fox_w_out):
    d = x_prompt.shape[-1]
    m_heads, dqk, dv = state_mlstm_C.shape[2:]
    f_heads = cache_fox_lf.shape[-1]
    qk, vd = m_heads * dqk, m_heads * dv
    n_mlstm, n_fox = mlstm_w_in.shape[0], fox_w_in.shape[0]
    w = {
        "m_heads": m_heads, "dqk": dqk, "dv": dv, "f_heads": f_heads,
        "norm_gains": norm_gains, "mlstm_b_gates": mlstm_b_gates, "mlstm_g_h": mlstm_g_h,
        "fox_b_f": fox_b_f, "fox_g_qk": fox_g_qk,
        "ffn1_in": ffn1_in.astype(BF16), "ffn1_out": ffn1_out.astype(BF16),
        "ffn2_in": ffn2_in.astype(BF16), "ffn2_out": ffn2_out.astype(BF16),
        "ple_gate": ple_gate.astype(BF16), "ple_proj": ple_proj.astype(BF16),
        "mlstm_w_in": mlstm_w_in.astype(BF16), "mlstm_w_out": mlstm_w_out.astype(BF16),
        "fox_w_in": fox_w_in.astype(BF16), "fox_w_out": fox_w_out.astype(BF16),
        "mlstm_gate_rows": [_gate_weight_rows(mlstm_w_in[j], 2 * qk + 2 * vd, 2 * m_heads)
                            for j in range(n_mlstm)],
        "fox_gate_rows": [_gate_weight_rows(fox_w_in[j], 4 * d, f_heads) for j in range(n_fox)],
    }
    kp = cache_fox_k.reshape(cache_fox_k.shape[:3] + (d,))
    vp = cache_fox_v.reshape(cache_fox_v.shape[:3] + (d,))
    y_p, p_c, p_n, p_m, p_k, p_v, p_lf = _run_trunk(x_prompt, p_prompt, w, None, None)
    y_s, s_c, s_n, s_m, s_k, s_v, s_lf = _run_trunk(
        x_sample, p_sample, w, (state_mlstm_C, state_mlstm_n, state_mlstm_m), (kp, vp, cache_fox_lf))
    return (y_p, y_s, p_c, p_n, p_m, p_k, p_v, p_lf, s_c, s_n, s_m, s_k, s_v, s_lf)
```

```python
import functools

import jax
import jax.numpy as jnp
from jax import lax
from jax.experimental import pallas as pl
from jax.experimental.pallas import tpu as pltpu

F32 = jnp.float32
BF16 = jnp.bfloat16
NORM_EPS = 1e-6
FFN_HALF = 0.5
V7X_VMEM_BYTES = 64 * 1024 * 1024
VMEM_LIMIT = V7X_VMEM_BYTES - 8 * 1024 * 1024
LANES = 128
SUBLANES = 8
NEG_INF = float("-inf")
LOG2E = 1.4426950408889634
ATTN_KEY_BATCH = 4


def _params(*sem):
    return pltpu.CompilerParams(dimension_semantics=sem, vmem_limit_bytes=VMEM_LIMIT)


def _tile(n, pref):
    if n <= pref:
        return n
    for t in range(pref - pref % LANES, 0, -LANES):
        if n % t == 0:
            return t
    raise ValueError((n, pref))


def _rms(x, g):
    ms = jnp.mean(x * x, axis=-1, keepdims=True)
    return x * lax.rsqrt(ms + NORM_EPS) * g


def _sigmoid(x):
    return 1.0 / (1.0 + jnp.exp(-x))


def _log_sigmoid(x):
    return jnp.minimum(x, 0.0) - jnp.log(1.0 + jnp.exp(-jnp.abs(x)))


def _dot(a, b):
    return jnp.dot(a, b, preferred_element_type=F32)


def _dot_nt(a, b):
    return lax.dot_general(a, b, (((1,), (1,)), ((), ())), preferred_element_type=F32)


def _dot_tn(a, b):
    return lax.dot_general(a, b, (((0,), (0,)), ((), ())), preferred_element_type=F32)


def _dot_f32(a, b):
    return jnp.dot(a, b, preferred_element_type=F32, precision=lax.Precision.HIGHEST)


def _ffn_body(x_ref, g_ref, wg_ref, wu_ref, wo_ref, out_ref, xn_ref, *, col_chunk):
    f = pl.program_id(1)

    @pl.when(f == 0)
    def _():
        x = x_ref[...]
        xn_ref[...] = _rms(x, g_ref[...]).astype(BF16)
        out_ref[...] = x

    xn = xn_ref[...]
    g = _dot(xn, wg_ref[...])
    u = _dot(xn, wu_ref[...])
    h = (g * _sigmoid(g) * u * FFN_HALF).astype(BF16)
    d = out_ref.shape[1]
    for c in range(d // col_chunk):
        sl = slice(c * col_chunk, (c + 1) * col_chunk)
        out_ref[:, sl] += _dot(h, wo_ref[:, sl])


def _ffn(x, gain, w_in, w_out, layer):
    m, d = x.shape
    f_dim = w_out.shape[1]
    tm = _tile(m, 1024)
    tf = _tile(f_dim, 512)
    nf = f_dim // tf
    return pl.pallas_call(
        functools.partial(_ffn_body, col_chunk=_tile(d, 512)),
        out_shape=jax.ShapeDtypeStruct((m, d), F32),
        grid=(m // tm, nf),
        in_specs=[
            pl.BlockSpec((tm, d), lambda i, f: (i, 0), pipeline_mode=pl.Buffered(1)),
            pl.BlockSpec((1, d), lambda i, f: (0, 0)),
            pl.BlockSpec((None, d, tf), lambda i, f: (layer, 0, f)),
            pl.BlockSpec((None, d, tf), lambda i, f: (layer, 0, f + nf)),
            pl.BlockSpec((None, tf, d), lambda i, f: (layer, f, 0)),
        ],
        out_specs=pl.BlockSpec((tm, d), lambda i, f: (i, 0)),
        scratch_shapes=[pltpu.VMEM((tm, d), BF16)],
        compiler_params=_params("parallel", "arbitrary"),
        name="ffn",
    )(x, gain, w_in, w_in, w_out)


def _proj_body(*refs, mode, hd, scale):
    x_ref, g_ref, w_ref = refs[:3]
    xn_ref = refs[-1]
    j = pl.program_id(1)

    @pl.when(j == 0)
    def _():
        xn_ref[...] = _rms(x_ref[...], g_ref[...]).astype(BF16)

    z = _dot(xn_ref[...], w_ref[...])
    if mode == "plain":
        refs[3][...] = z
    elif mode == "dual":
        refs[3][...] = z
        refs[4][...] = z.astype(BF16)
    else:
        gh_ref = refs[3]
        outs = refs[4:-1]
        for c in range(z.shape[1] // hd):
            sl = slice(c * hd, (c + 1) * hd)
            y = _rms(z[:, sl], gh_ref[...])
            if mode == "headnorm_q":
                outs[0][:, sl] = (y * scale).astype(BF16)
            else:
                outs[0][:, sl] = y
                outs[1][:, sl] = y.astype(BF16)


def _proj(x, gain, w, layer, col0, ncols, mode, head_gain=None, hd=None, scale=None):
    m, d = x.shape
    tm = _tile(m, 1024)
    tn = _tile(ncols, 1024)
    assert col0 % tn == 0
    jb = col0 // tn
    in_specs = [
        pl.BlockSpec((tm, d), lambda i, j: (i, 0), pipeline_mode=pl.Buffered(1)),
        pl.BlockSpec((1, d), lambda i, j: (0, 0)),
        pl.BlockSpec((None, d, tn), lambda i, j: (layer, 0, jb + j)),
    ]
    args = [x, gain, w]
    o_spec = pl.BlockSpec((tm, tn), lambda i, j: (i, j))
    if mode == "plain":
        out_shape = jax.ShapeDtypeStruct((m, ncols), F32)
        out_specs = o_spec
    elif mode == "headnorm_q":
        out_shape = jax.ShapeDtypeStruct((m, ncols), BF16)
        out_specs = o_spec
    else:
        out_shape = (jax.ShapeDtypeStruct((m, ncols), F32), jax.ShapeDtypeStruct((m, ncols), BF16))
        out_specs = (o_spec, o_spec)
    if mode.startswith("headnorm"):
        in_specs.append(pl.BlockSpec((1, hd), lambda i, j: (0, 0)))
        args.append(head_gain)
    return pl.pallas_call(
        functools.partial(_proj_body, mode=mode, hd=hd, scale=scale),
        out_shape=out_shape,
        grid=(m // tm, ncols // tn),
        in_specs=in_specs,
        out_specs=out_specs,
        scratch_shapes=[pltpu.VMEM((tm, d), BF16)],
        compiler_params=_params("parallel", "arbitrary"),
        name="proj_" + mode,
    )(*args)


def _gate_body(x_ref, g_ref, wt_ref, o_ref):
    xn = _rms(x_ref[...], g_ref[...]).astype(BF16)
    o_ref[...] = _dot_nt(wt_ref[...], xn)


def _gate_proj(x, gain, wt):
    m, d = x.shape
    ng = wt.shape[0]
    tm = _tile(m, 1024)
    return pl.pallas_call(
        _gate_body,
        out_shape=jax.ShapeDtypeStruct((ng, m), F32),
        grid=(m // tm,),
        in_specs=[
            pl.BlockSpec((tm, d), lambda i: (i, 0)),
            pl.BlockSpec((1, d), lambda i: (0, 0)),
            pl.BlockSpec((ng, d), lambda i: (0, 0)),
        ],
        out_specs=pl.BlockSpec((ng, tm), lambda i: (0, i)),
        compiler_params=_params("parallel"),
        name="gate_proj",
    )(x, gain, wt)


def _out_body(a_ref, w_ref, x_ref, o_ref):
    o_ref[...] = x_ref[...] + _dot(a_ref[...], w_ref[...])


def _out_proj(a, w, x, layer):
    m, k = a.shape
    d = x.shape[1]
    tm = _tile(m, 1024)
    tn = _tile(d, 1024)
    return pl.pallas_call(
        _out_body,
        out_shape=jax.ShapeDtypeStruct((m, d), F32),
        grid=(m // tm, d // tn),
        in_specs=[
            pl.BlockSpec((tm, k), lambda i, j: (i, 0)),
            pl.BlockSpec((None, k, tn), lambda i, j: (layer, 0, j)),
            pl.BlockSpec((tm, tn), lambda i, j: (i, j)),
        ],
        out_specs=pl.BlockSpec((tm, tn), lambda i, j: (i, j)),
        compiler_params=_params("parallel", "arbitrary"),
        name="out_proj",
    )(a, w, x)


def _ple_body(x_ref, g_ref, wg_ref, p_ref, wp_ref, xr_ref, o_ref, xn_ref):
    j = pl.program_id(1)

    @pl.when(j == 0)
    def _():
        xn_ref[...] = _rms(x_ref[...], g_ref[...]).astype(BF16)

    gate = _sigmoid(_dot(xn_ref[...], wg_ref[...]))
    emb = _dot(p_ref[...].astype(BF16), wp_ref[...])
    o_ref[...] = xr_ref[...] + gate * emb


def _ple(x, gain, w_gate, p, w_proj, layer):
    m, d = x.shape
    pd = p.shape[-1]
    tm = _tile(m, 1024)
    tn = _tile(d, 1024)
    return pl.pallas_call(
        _ple_body,
        out_shape=jax.ShapeDtypeStruct((m, d), F32),
        grid=(m // tm, d // tn),
        in_specs=[
            pl.BlockSpec((tm, d), lambda i, j: (i, 0), pipeline_mode=pl.Buffered(1)),
            pl.BlockSpec((1, d), lambda i, j: (0, 0)),
            pl.BlockSpec((None, d, tn), lambda i, j: (layer, 0, j)),
            pl.BlockSpec((None, tm, pd), lambda i, j: (layer, i, 0)),
            pl.BlockSpec((None, pd, tn), lambda i, j: (layer, 0, j)),
            pl.BlockSpec((tm, tn), lambda i, j: (i, j)),
        ],
        out_specs=pl.BlockSpec((tm, tn), lambda i, j: (i, j)),
        scratch_shapes=[pltpu.VMEM((tm, d), BF16)],
        compiler_params=_params("parallel", "arbitrary"),
        name="ple",
    )(x, gain, w_gate, p, w_proj, x)


def _mlstm_body(q_ref, k_ref, v_ref, o_ref, gc_ref, gr_ref, bc_ref, br_ref, gh_ref,
                c0_ref, n0_ref, m0_ref, a_ref, c_ref, n_ref, m_ref, *, heads, dqk, dv):
    ci = pl.program_id(1)
    L = q_ref.shape[0]

    @pl.when(ci == 0)
    def _():
        c_ref[...] = c0_ref[...]
        n_ref[...] = n0_ref[...]
        m_ref[...] = m0_ref[...]

    row = lax.broadcasted_iota(jnp.int32, (L, L), 0)
    col = lax.broadcasted_iota(jnp.int32, (L, L), 1)
    causal = row >= col
    tril = causal.astype(F32)
    triu = (row <= col).astype(F32)

    gc = gc_ref[...] + bc_ref[...]
    gr = gr_ref[...] + br_ref[...]
    b_col = _dot_f32(tril, _log_sigmoid(gc))
    b_row = _dot_f32(_log_sigmoid(gr), triu)
    scale = dqk ** -0.5

    for h in range(heads):
        q = q_ref[:, h * dqk:(h + 1) * dqk]
        k = k_ref[:, h * dqk:(h + 1) * dqk] * scale
        v = v_ref[:, h * dv:(h + 1) * dv].astype(BF16)
        qb = q.astype(BF16)
        cst = c_ref[h]
        nst = n_ref[h]
        m_prev = m_ref[h][:, :1]

        bc = b_col[:, heads + h:heads + h + 1]
        ic = gc[:, h:h + 1]
        brow = b_row[heads + h:heads + h + 1, :]
        irow = gr[h:h + 1, :]

        dmat = jnp.where(causal, bc - brow + irow, NEG_INF)
        g = bc + m_prev
        m_t = jnp.maximum(g, jnp.max(dmat, axis=-1, keepdims=True))
        s = _dot_nt(qb, k.astype(BF16)) * jnp.exp(dmat - m_t)
        inter = jnp.exp(g - m_t)
        num = _dot(s.astype(BF16), v) + inter * _dot(qb, cst.astype(BF16))
        den = jnp.sum(s, axis=-1, keepdims=True) + inter * jnp.sum(q * nst, axis=-1, keepdims=True)
        hval = num / jnp.maximum(jnp.abs(den), jnp.exp(-m_t))

        hn = _rms(hval, gh_ref[:, h * dv:(h + 1) * dv])
        og = o_ref[:, h * dv:(h + 1) * dv]
        a_ref[:, h * dv:(h + 1) * dv] = (hn * _sigmoid(og)).astype(BF16)

        b_end = bc[L - 1:L, :]
        m_new = m_t[L - 1:L, :]
        decay = jnp.exp(b_end + m_prev - m_new)
        wa = jnp.exp(b_end - bc + ic - m_new)
        kw = k * wa
        c_ref[h] = decay * cst + _dot_tn(kw.astype(BF16), v)
        n_ref[h] = decay * nst + jnp.sum(kw, axis=0, keepdims=True)
        m_ref[h] = jnp.broadcast_to(m_new, (1, LANES))


def _mlstm(z, gates_col, gates_row, bias_lanes, bias_rows, g_h, c0, n0, m0, chunk):
    b, t, _ = z.shape
    heads, dqk, dv = c0.shape[1:]
    qk = heads * dqk
    vd = heads * dv
    assert (2 * qk) % vd == 0
    v_blk = 2 * qk // vd
    ng = gates_row.shape[1]
    nc = t // chunk
    st4 = lambda i, c: (i, 0, 0, 0)
    return pl.pallas_call(
        functools.partial(_mlstm_body, heads=heads, dqk=dqk, dv=dv),
        out_shape=(
            jax.ShapeDtypeStruct((b, t, vd), BF16),
            jax.ShapeDtypeStruct((b, heads, dqk, dv), F32),
            jax.ShapeDtypeStruct((b, heads, 1, dqk), F32),
            jax.ShapeDtypeStruct((b, heads, 1, LANES), F32),
        ),
        grid=(b, nc),
        in_specs=[
            pl.BlockSpec((None, chunk, qk), lambda i, c: (i, c, 0)),
            pl.BlockSpec((None, chunk, qk), lambda i, c: (i, c, 1)),
            pl.BlockSpec((None, chunk, vd), lambda i, c: (i, c, v_blk)),
            pl.BlockSpec((None, chunk, vd), lambda i, c: (i, c, v_blk + 1)),
            pl.BlockSpec((None, chunk, LANES), lambda i, c: (i, c, 0)),
            pl.BlockSpec((None, ng, chunk), lambda i, c: (i, 0, c)),
            pl.BlockSpec((1, LANES), lambda i, c: (0, 0)),
            pl.BlockSpec((ng, 1), lambda i, c: (0, 0)),
            pl.BlockSpec((1, vd), lambda i, c: (0, 0)),
            pl.BlockSpec((None, heads, dqk, dv), st4),
            pl.BlockSpec((None, heads, 1, dqk), st4),
            pl.BlockSpec((None, heads, 1, LANES), st4),
        ],
        out_specs=(
            pl.BlockSpec((None, chunk, vd), lambda i, c: (i, c, 0)),
            pl.BlockSpec((None, heads, dqk, dv), st4),
            pl.BlockSpec((None, heads, 1, dqk), st4),
            pl.BlockSpec((None, heads, 1, LANES), st4),
        ),
        compiler_params=_params("parallel", "arbitrary"),
        name="mlstm",
    )(z, z, z, z, gates_col, gates_row, bias_lanes, bias_rows, g_h, c0, n0, m0)


def _cumsum_body(f_ref, b_ref, lf_ref, c_ref, *, blk, activate, anchor_end):
    t = f_ref.shape[1]
    row = lax.broadcasted_iota(jnp.int32, (blk, blk), 0)
    col = lax.broadcasted_iota(jnp.int32, (blk, blk), 1)
    triu = (row <= col).astype(F32)
    carry = jnp.zeros((f_ref.shape[0], 1), F32)
    for i in range(t // blk):
        sl = slice(i * blk, (i + 1) * blk)
        seg = f_ref[:, sl]
        if activate:
            seg = _log_sigmoid(seg + b_ref[...])
        lf_ref[:, sl] = seg
        cs = _dot_f32(seg, triu) + carry
        c_ref[:, sl] = cs
        carry = cs[:, blk - 1:blk]
    if anchor_end:
        c_ref[...] = c_ref[...] - carry


def _forget_cumsum(f_rows, bias_col, activate, anchor_end):
    b, ng, t = f_rows.shape
    blk = _tile(t, 512)
    spec = pl.BlockSpec((None, ng, t), lambda i: (i, 0, 0))
    return pl.pallas_call(
        functools.partial(_cumsum_body, blk=blk, activate=activate, anchor_end=anchor_end),
        out_shape=(jax.ShapeDtypeStruct((b, ng, t), F32), jax.ShapeDtypeStruct((b, ng, t), F32)),
        grid=(b,),
        in_specs=[spec, pl.BlockSpec((ng, 1), lambda i: (0, 0))],
        out_specs=(spec, spec),
        compiler_params=_params("parallel"),
        name="forget_cumsum",
    )(f_rows, bias_col)


def _split3(c):
    hi = c.astype(BF16).astype(F32)
    r = c - hi
    mid = r.astype(BF16).astype(F32)
    return hi, mid, r - mid


def _bias_lanes(c, query_side):
    hi, mid, lo = _split3(c if query_side else -c)
    lane = lax.broadcasted_iota(jnp.int32, (c.shape[0], LANES), 1)
    o = 0 if query_side else 3
    terms = jnp.where(lane == o, hi, jnp.where(lane == o + 1, mid, jnp.where(lane == o + 2, lo, 0.0)))
    ones = jnp.logical_and(lane >= 3 - o, lane < 6 - o)
    return jnp.where(ones, 1.0, terms).astype(BF16)


def _online_softmax_step(s, v, m, l, acc):
    m_new = jnp.maximum(m, jnp.max(s, axis=-1, keepdims=True))
    p = jnp.exp2(s - m_new)
    alpha = jnp.exp2(m - m_new)
    l = alpha * l + jnp.sum(p, axis=-1, keepdims=True)
    acc = alpha * acc + _dot(p.astype(BF16), v)
    return m_new, l, acc


def _bias_rows(c, rows):
    hi, mid, lo = _split3(c)
    sub = lax.broadcasted_iota(jnp.int32, (rows, c.shape[1]), 0)
    terms = jnp.where(sub == 0, hi, jnp.where(sub == 1, mid, jnp.where(sub == 2, lo, 0.0)))
    return jnp.where(jnp.logical_and(sub >= 3, sub < 6), 1.0, terms).astype(BF16)


def _fox_prompt_body(q_ref, k_ref, v_ref, og_ref, cc_ref, cr_ref, a_ref, qt_ref, ka_ref, vt_ref, *, tq,
                     kbatch):
    h = pl.program_id(1)
    t, hd = q_ref.shape
    nt = t // tq
    for r in range(nt):
        rows = slice(r * tq, (r + 1) * tq)
        cc = cc_ref[rows, :]
        lane = lax.broadcasted_iota(jnp.int32, cc.shape, 1)
        c_col = jnp.sum(jnp.where(lane == h, cc, 0.0), axis=-1, keepdims=True) * LOG2E
        ka_ref[rows, :hd] = k_ref[rows, :]
        ka_ref[rows, hd:] = _bias_lanes(c_col, False)
        qt_ref[:hd, rows] = jnp.transpose(q_ref[rows, :].astype(F32)).astype(BF16)
        qt_ref[hd:, rows] = _bias_rows(cr_ref[r:r + 1, :] * LOG2E, LANES)
        vt_ref[:, rows] = jnp.transpose(v_ref[rows, :].astype(F32)).astype(BF16)

    key = lax.broadcasted_iota(jnp.int32, (tq, tq), 0)
    qry = lax.broadcasted_iota(jnp.int32, (tq, tq), 1)
    causal = key <= qry

    def tile_off(kj):
        return kj * tq if isinstance(kj, int) else pl.multiple_of(kj * tq, tq)

    for qi in range(nt):
        cols = slice(qi * tq, (qi + 1) * tq)
        qt = qt_ref[:, cols]

        def steps(carry, tiles):
            m, l, acc = carry
            scores = [_dot(ka_ref[pl.ds(tile_off(kj), tq), :], qt) for kj, _ in tiles]
            for s, (kj, diagonal) in zip(scores, tiles):
                if diagonal:
                    s = jnp.where(causal, s, NEG_INF)
                m_new = jnp.maximum(m, jnp.max(s, axis=0, keepdims=True))
                p = jnp.exp2(s - m_new)
                alpha = jnp.exp2(m - m_new)
                l = alpha * l + jnp.sum(p, axis=0, keepdims=True)
                acc = alpha * acc + _dot(vt_ref[:, pl.ds(tile_off(kj), tq)], p.astype(BF16))
                m = m_new
            return m, l, acc

        carry = (jnp.full((1, tq), NEG_INF, F32), jnp.zeros((1, tq), F32), jnp.zeros((hd, tq), F32))
        nb = qi // kbatch
        carry = lax.fori_loop(
            0, nb, lambda i, c: steps(c, [(i * kbatch + u, False) for u in range(kbatch)]), carry)
        tail = [(kj, False) for kj in range(nb * kbatch, qi)] + [(qi, True)]
        _, l, acc = steps(carry, tail)
        o = jnp.transpose(acc / l)
        a_ref[cols, :] = (o * _sigmoid(og_ref[cols, :])).astype(BF16)


def _fox_prompt_attn(qb, kb, vb, og, c_col, c_row, heads):
    b, t, d = qb.shape
    hd = d // heads
    tq = _tile(t, 512)
    nt = t // tq
    c_row4 = c_row.reshape(b, c_row.shape[1], nt, tq)
    head_blk = pl.BlockSpec((None, t, hd), lambda i, h: (i, 0, h))
    return pl.pallas_call(
        functools.partial(_fox_prompt_body, tq=tq, kbatch=ATTN_KEY_BATCH),
        out_shape=jax.ShapeDtypeStruct((b, t, d), BF16),
        grid=(b, heads),
        in_specs=[
            head_blk, head_blk, head_blk, head_blk,
            pl.BlockSpec((None, t, LANES), lambda i, h: (i, 0, 0)),
            pl.BlockSpec((None, None, nt, tq), lambda i, h: (i, h, 0, 0)),
        ],
        out_specs=head_blk,
        scratch_shapes=[pltpu.VMEM((hd + LANES, t), BF16), pltpu.VMEM((t, hd + LANES), BF16),
                        pltpu.VMEM((hd, t), BF16)],
        compiler_params=_params("parallel", "arbitrary"),
        name="fox_prompt_attn",
    )(qb, kb, vb, og, c_col, c_row4)


def _key_bias_lanes(c3, heads):
    lane = lax.broadcasted_iota(jnp.int32, c3.shape, 1)
    hi, mid, lo = _split3(c3 * jnp.where(lane < 3 * heads, LOG2E, 1.0))
    pick = jnp.where(lane < heads, hi, jnp.where(lane < 2 * heads, mid, jnp.where(lane < 3 * heads, lo, hi)))
    return pick.astype(BF16)


def _fox_sample_body(q_ref, kp_ref, vp_ref, kn_ref, vn_ref, og_ref, cq_ref, c3p_ref, c3n_ref,
                     a_ref, wq_ref, m_ref, l_ref, acc_ref, *, heads, hd, hg):
    j = pl.program_id(1)
    nj = pl.num_programs(1)
    tq = q_ref.shape[0]
    groups = heads // hg
    tk = kp_ref.shape[0] // heads

    @pl.when(j == 0)
    def _():
        m_ref[...] = jnp.full(m_ref.shape, NEG_INF, F32)
        l_ref[...] = jnp.zeros(l_ref.shape, F32)
        acc_ref[...] = jnp.zeros(acc_ref.shape, F32)
        wq_ref[...] = jnp.zeros(wq_ref.shape, BF16)
        lane = lax.broadcasted_iota(jnp.int32, (tq, LANES), 1)
        for g in range(groups):
            for i in range(hg):
                h = g * hg + i
                rows = slice(i * tq, (i + 1) * tq)
                wq_ref[g, rows, i * hd:(i + 1) * hd] = q_ref[:, h * hd:(h + 1) * hd]
                hi, mid, lo = _split3(cq_ref[:, h:h + 1] * LOG2E)
                own = jnp.logical_or(lane == h, jnp.logical_or(lane == heads + h, lane == 2 * heads + h))
                cq3 = jnp.where(lane == 3 * heads, hi,
                                jnp.where(lane == 3 * heads + 1, mid,
                                          jnp.where(lane == 3 * heads + 2, lo, 0.0)))
                wq_ref[g, rows, hg * hd:] = jnp.where(own, -1.0, cq3).astype(BF16)

    def update(g, s, v):
        m, l, acc = _online_softmax_step(s, v, m_ref[g], l_ref[g], acc_ref[g])
        m_ref[g] = m
        l_ref[g] = l
        acc_ref[g] = acc

    kbias = _key_bias_lanes(c3p_ref[...], heads)
    for g in range(groups):
        head_rows = lambda i: pl.ds(g * hg + i, tk, stride=heads)
        ks = [kp_ref[head_rows(i), :].astype(BF16) for i in range(hg)]
        vs = [vp_ref[head_rows(i), :].astype(BF16) for i in range(hg)]
        s = _dot_nt(wq_ref[g], jnp.concatenate(ks + [kbias], axis=1))
        update(g, s, jnp.concatenate(vs, axis=1))

    @pl.when(j == nj - 1)
    def _():
        row = lax.broadcasted_iota(jnp.int32, (hg * tq, tq), 0)
        col = lax.broadcasted_iota(jnp.int32, (hg * tq, tq), 1)
        causal = row % tq >= col
        nbias = _key_bias_lanes(c3n_ref[...], heads)
        for g in range(groups):
            cols = slice(g * hg * hd, (g + 1) * hg * hd)
            s = _dot_nt(wq_ref[g], jnp.concatenate([kn_ref[:, cols], nbias], axis=1))
            update(g, jnp.where(causal, s, NEG_INF), vn_ref[:, cols])
            for i in range(hg):
                rows = slice(i * tq, (i + 1) * tq)
                sl = slice((g * hg + i) * hd, (g * hg + i + 1) * hd)
                o = acc_ref[g, rows, i * hd:(i + 1) * hd] / l_ref[g, rows, :]
                a_ref[:, sl] = (o * _sigmoid(og_ref[:, sl])).astype(BF16)


def _fox_sample_attn(qb, k_past, v_past, layer, kb, vb, og, cq_col, c3_past, c3_new):
    b, tq, d = qb.shape
    p, heads, hd = k_past.shape[2:]
    assert 3 * heads + 3 <= LANES
    hg = max(1, min(heads, 512 // hd))
    assert heads % hg == 0
    tk = _tile(p, 512)
    new_blk = pl.BlockSpec((None, tq, d), lambda i, j: (i, 0, 0))
    past_blk = pl.BlockSpec((None, None, tk * heads, hd), lambda i, j: (layer, i, j, 0))
    flat = lambda a: a.reshape(a.shape[:2] + (p * heads, hd))
    return pl.pallas_call(
        functools.partial(_fox_sample_body, heads=heads, hd=hd, hg=hg),
        out_shape=jax.ShapeDtypeStruct((b, tq, d), BF16),
        grid=(b, p // tk),
        in_specs=[
            new_blk, past_blk, past_blk, new_blk, new_blk, new_blk,
            pl.BlockSpec((None, tq, LANES), lambda i, j: (i, 0, 0)),
            pl.BlockSpec((None, tk, LANES), lambda i, j: (i, j, 0)),
            pl.BlockSpec((None, tq, LANES), lambda i, j: (i, 0, 0)),
        ],
        out_specs=new_blk,
        scratch_shapes=[
            pltpu.VMEM((heads // hg, hg * tq, hg * hd + LANES), BF16),
            pltpu.VMEM((heads // hg, hg * tq, 1), F32),
            pltpu.VMEM((heads // hg, hg * tq, 1), F32),
            pltpu.VMEM((heads // hg, hg * tq, hg * hd), F32),
        ],
        compiler_params=_params("parallel", "arbitrary"),
        name="fox_sample_attn",
    )(qb, flat(k_past), flat(v_past), kb, vb, og, cq_col, c3_past, c3_new)


def _round_up(n, k):
    return (n + k - 1) // k * k


def _gate_weight_rows(w, col0, n):
    wt = jnp.transpose(w[:, col0:col0 + n]).astype(BF16)
    return jnp.pad(wt, ((0, _round_up(n, SUBLANES) - n), (0, 0)))


def _pad_rows(a, rows):
    return jnp.pad(a, ((0, rows - a.shape[0]),) + ((0, 0),) * (a.ndim - 1))


def _col_layout(rows):
    c = jnp.swapaxes(rows, 1, 2)
    return jnp.pad(c, ((0, 0), (0, 0), (0, LANES - c.shape[-1])))


def _c3_layout(rows, heads):
    c = jnp.swapaxes(rows[:, :heads, :], 1, 2)
    c3 = jnp.concatenate([c, c, c, jnp.ones(c.shape[:2] + (3,), F32)], axis=-1)
    return jnp.pad(c3, ((0, 0), (0, 0), (0, LANES - c3.shape[-1])))


def _run_trunk(x, p, w, mlstm_init, fox_past):
    b, t, d = x.shape
    depth = w["norm_gains"].shape[0]
    m_heads, dqk, dv = w["m_heads"], w["dqk"], w["dv"]
    f_heads = w["f_heads"]
    hd = d // f_heads
    qk, vd = m_heads * dqk, m_heads * dv
    mtok = b * t
    xf = x.reshape(mtok, d)
    pf = p.reshape(depth, mtok, p.shape[-1])
    chunk = _tile(t, 256)
    m_states, f_rows = [], []
    for i in range(depth):
        gains = w["norm_gains"][i]
        gain = lambda r: gains[r].reshape(1, d)
        j = i // 2
        xf = _ffn(xf, gain(0), w["ffn1_in"], w["ffn1_out"], i)
        if i % 2 == 0:
            z = _proj(xf, gain(1), w["mlstm_w_in"], j, 0, 2 * qk + 2 * vd, "plain")
            g_rows = _gate_proj(xf, gain(1), w["mlstm_gate_rows"][j])
            ng = g_rows.shape[0]
            g_rows = jnp.swapaxes(g_rows.reshape(ng, b, t), 0, 1)
            if mlstm_init is None:
                c0 = jnp.zeros((b, m_heads, dqk, dv), F32)
                n0 = jnp.zeros((b, m_heads, dqk), F32)
                m0 = jnp.zeros((b, m_heads), F32)
            else:
                c0, n0, m0 = mlstm_init[0][j], mlstm_init[1][j], mlstm_init[2][j]
            bias = w["mlstm_b_gates"][j].reshape(2 * m_heads)
            a, c_new, n_new, m_new = _mlstm(
                z.reshape(b, t, -1), _col_layout(g_rows), g_rows,
                jnp.pad(bias, (0, LANES - 2 * m_heads)).reshape(1, LANES),
                _pad_rows(bias.reshape(-1, 1), ng),
                w["mlstm_g_h"][j].reshape(1, vd),
                c0, n0.reshape(b, m_heads, 1, dqk),
                jnp.broadcast_to(m0[:, :, None, None], (b, m_heads, 1, LANES)), chunk)
            m_states.append((c_new, n_new.reshape(b, m_heads, dqk), m_new[:, :, 0, 0]))
            xf = _out_proj(a.reshape(mtok, vd), w["mlstm_w_out"], xf, j)
        else:
            g_qk = w["fox_g_qk"][j]
            qb = _proj(xf, gain(1), w["fox_w_in"], j, 0, d, "headnorm_q",
                       head_gain=g_qk[0].reshape(1, hd), hd=hd, scale=hd ** -0.5 * LOG2E)
            kf, kb = _proj(xf, gain(1), w["fox_w_in"], j, d, d, "headnorm_kv",
                           head_gain=g_qk[1].reshape(1, hd), hd=hd)
            vf, vb = _proj(xf, gain(1), w["fox_w_in"], j, 2 * d, d, "dual")
            og = _proj(xf, gain(1), w["fox_w_in"], j, 3 * d, d, "plain")
            f_pre = _gate_proj(xf, gain(1), w["fox_gate_rows"][j])
            ng = f_pre.shape[0]
            f_pre = jnp.swapaxes(f_pre.reshape(ng, b, t), 0, 1)
            bias_col = _pad_rows(w["fox_b_f"][j].reshape(-1, 1), ng)
            lf_row, c_row = _forget_cumsum(f_pre, bias_col, True, False)
            shp = (b, t, d)
            if fox_past is None:
                a = _fox_prompt_attn(qb.reshape(shp), kb.reshape(shp), vb.reshape(shp),
                                     og.reshape(shp), _col_layout(c_row), c_row, f_heads)
            else:
                lf_past_row = jnp.swapaxes(fox_past[2][j].astype(F32), 1, 2)
                lf_past_row = jnp.pad(lf_past_row, ((0, 0), (0, ng - f_heads), (0, 0)))
                _, c_past_row = _forget_cumsum(lf_past_row, bias_col, False, True)
                a = _fox_sample_attn(qb.reshape(shp), fox_past[0], fox_past[1], j,
                                     kb.reshape(shp), vb.reshape(shp), og.reshape(shp),
                                     _col_layout(c_row), _c3_layout(c_past_row, f_heads),
                                     _c3_layout(c_row, f_heads))
            lf = jnp.swapaxes(lf_row[:, :f_heads, :], 1, 2)
            f_rows.append((kf.reshape(b, t, f_heads, hd), vf.reshape(b, t, f_heads, hd), lf))
            xf = _out_proj(a.reshape(mtok, d), w["fox_w_out"], xf, j)
        xf = _ffn(xf, gain(2), w["ffn2_in"], w["ffn2_out"], i)
        xf = _ple(xf, gain(3), w["ple_gate"], pf, w["ple_proj"], i)
    stack = lambda k: jnp.stack([s[k] for s in m_states])
    stack_f = lambda k: jnp.stack([r[k] for r in f_rows])
    return (xf.reshape(b, t, d), stack(0), stack(1), stack(2), stack_f(0), stack_f(1), stack_f(2))


def kernel(x_prompt, x_sample, p_prompt, p_sample, state_mlstm_C, state_mlstm_n, state_mlstm_m, cache_fox_k, cache_fox_v, cache_fox_lf, norm_gains, ffn1_in, ffn1_out, ffn2_in, ffn2_out, ple_gate, ple_proj, mlstm_w_in, mlstm_b_gates, mlstm_g_h, mlstm_w_out, fox_w_in, fox_b_f, fox_g_qk, fox_w_out):
    d = x_prompt.shape[-1]
    m_heads, dqk, dv = state_mlstm_C.shape[2:]
    f_heads = cache_fox_lf.shape[-1]
    qk, vd = m_heads * dqk, m_heads * dv
    n_mlstm, n_fox = mlstm_w_in.shape[0], fox_w_in.shape[0]
    w = {
        "m_heads": m_heads, "dqk": dqk, "dv": dv, "f_heads": f_heads,
        "norm_gains": norm_gains, "mlstm_b_gates": mlstm_b_gates, "mlstm_g_h": mlstm_g_h,
        "fox_b_f": fox_b_f, "fox_g_qk": fox_g_qk,
        "ffn1_in": ffn1_in.astype(BF16), "ffn1_out": ffn1_out.astype(BF16),
        "ffn2_in": ffn2_in.astype(BF16), "ffn2_out": ffn2_out.astype(BF16),
        "ple_gate": ple_gate.astype(BF16), "ple_proj": ple_proj.astype(BF16),
        "mlstm_w_in": mlstm_w_in.astype(BF16), "mlstm_w_out": mlstm_w_out.astype(BF16),
        "fox_w_in": fox_w_in.astype(BF16), "fox_w_out": fox_w_out.astype(BF16),
        "mlstm_gate_rows": [_gate_weight_rows(mlstm_w_in[j], 2 * qk + 2 * vd, 2 * m_heads)
                            for j in range(n_mlstm)],
        "fox_gate_rows": [_gate_weight_rows(fox_w_in[j], 4 * d, f_heads) for j in range(n_fox)],
    }
    y_p, p_c, p_n, p_m, p_k, p_v, p_lf = _run_trunk(x_prompt, p_prompt, w, None, None)
    y_s, s_c, s_n, s_m, s_k, s_v, s_lf = _run_trunk(
        x_sample, p_sample, w, (state_mlstm_C, state_mlstm_n, state_mlstm_m),
        (cache_fox_k, cache_fox_v, cache_fox_lf))
    return (y_p, y_s, p_c, p_n, p_m, p_k, p_v, p_lf, s_c, s_n, s_m, s_k, s_v, s_lf)
```

```python
import functools

import jax
import jax.numpy as jnp
from jax import lax
from jax.experimental import pallas as pl
from jax.experimental.pallas import tpu as pltpu

F32 = jnp.float32
BF16 = jnp.bfloat16
NORM_EPS = 1e-6
FFN_HALF = 0.5
V7X_VMEM_BYTES = 64 * 1024 * 1024
VMEM_LIMIT = V7X_VMEM_BYTES - 8 * 1024 * 1024
LANES = 128
SUBLANES = 8
NEG_INF = float("-inf")
LOG2E = 1.4426950408889634
ATTN_KEY_BATCH = 4


def _params(*sem):
    return pltpu.CompilerParams(dimension_semantics=sem, vmem_limit_bytes=VMEM_LIMIT)


def _tile(n, pref):
    if n <= pref:
        return n
    for t in range(pref - pref % LANES, 0, -LANES):
        if n % t == 0:
            return t
    raise ValueError((n, pref))


def _rms(x, g):
    ms = jnp.mean(x * x, axis=-1, keepdims=True)
    return x * lax.rsqrt(ms + NORM_EPS) * g


def _sigmoid(x):
    return 1.0 / (1.0 + jnp.exp(-x))


def _log_sigmoid(x):
    return jnp.minimum(x, 0.0) - jnp.log(1.0 + jnp.exp(-jnp.abs(x)))


def _dot(a, b):
    return jnp.dot(a, b, preferred_element_type=F32)


def _dot_nt(a, b):
    return lax.dot_general(a, b, (((1,), (1,)), ((), ())), preferred_element_type=F32)


def _dot_tn(a, b):
    return lax.dot_general(a, b, (((0,), (0,)), ((), ())), preferred_element_type=F32)


def _dot_f32(a, b):
    return jnp.dot(a, b, preferred_element_type=F32, precision=lax.Precision.HIGHEST)


def _ffn_body(x_ref, g_ref, wg_ref, wu_ref, wo_ref, out_ref, xn_ref, *, col_chunk):
    f = pl.program_id(1)

    @pl.when(f == 0)
    def _():
        x = x_ref[...]
        xn_ref[...] = _rms(x, g_ref[...]).astype(BF16)
        out_ref[...] = x

    xn = xn_ref[...]
    g = _dot(xn, wg_ref[...])
    u = _dot(xn, wu_ref[...])
    h = (g * _sigmoid(g) * u * FFN_HALF).astype(BF16)
    d = out_ref.shape[1]
    for c in range(d // col_chunk):
        sl = slice(c * col_chunk, (c + 1) * col_chunk)
        out_ref[:, sl] += _dot(h, wo_ref[:, sl])


def _ffn(x, gain, w_in, w_out, layer):
    m, d = x.shape
    f_dim = w_out.shape[1]
    tm = _tile(m, 1024)
    tf = _tile(f_dim, 512)
    nf = f_dim // tf
    return pl.pallas_call(
        functools.partial(_ffn_body, col_chunk=_tile(d, 512)),
        out_shape=jax.ShapeDtypeStruct((m, d), F32),
        grid=(m // tm, nf),
        in_specs=[
            pl.BlockSpec((tm, d), lambda i, f: (i, 0), pipeline_mode=pl.Buffered(1)),
            pl.BlockSpec((1, d), lambda i, f: (0, 0)),
            pl.BlockSpec((None, d, tf), lambda i, f: (layer, 0, f)),
            pl.BlockSpec((None, d, tf), lambda i, f: (layer, 0, f + nf)),
            pl.BlockSpec((None, tf, d), lambda i, f: (layer, f, 0)),
        ],
        out_specs=pl.BlockSpec((tm, d), lambda i, f: (i, 0)),
        scratch_shapes=[pltpu.VMEM((tm, d), BF16)],
        compiler_params=_params("parallel", "arbitrary"),
        name="ffn",
    )(x, gain, w_in, w_in, w_out)


def _rows_to_cols(rows):
    ng, n = rows.shape
    padded = jnp.concatenate([rows, jnp.zeros((LANES - ng, n), F32)], axis=0)
    return jnp.transpose(padded)


def _norm_and_gates(x_ref, g_ref, wt_ref, xn_ref, rows_ref, cols_ref):
    xn = _rms(x_ref[...], g_ref[...]).astype(BF16)
    xn_ref[...] = xn
    rows = _dot_nt(wt_ref[...], xn)
    rows_ref[...] = rows
    if cols_ref is not None:
        cols_ref[...] = _rows_to_cols(rows)


def _mlstm_in_body(x_ref, g_ref, w_ref, wt_ref, z_ref, rows_ref, cols_ref, xn_ref):
    @pl.when(pl.program_id(1) == 0)
    def _():
        _norm_and_gates(x_ref, g_ref, wt_ref, xn_ref, rows_ref, cols_ref)

    z_ref[...] = _dot(xn_ref[...], w_ref[...])


def _mlstm_in_proj(x, gain, w, wt, layer, ncols):
    m, d = x.shape
    ng = wt.shape[0]
    tm = _tile(m, 1024)
    tn = _tile(ncols, 1024)
    return pl.pallas_call(
        _mlstm_in_body,
        out_shape=(jax.ShapeDtypeStruct((m, ncols), F32), jax.ShapeDtypeStruct((ng, m), F32),
                   jax.ShapeDtypeStruct((m, LANES), F32)),
        grid=(m // tm, ncols // tn),
        in_specs=[
            pl.BlockSpec((tm, d), lambda i, j: (i, 0), pipeline_mode=pl.Buffered(1)),
            pl.BlockSpec((1, d), lambda i, j: (0, 0)),
            pl.BlockSpec((None, d, tn), lambda i, j: (layer, 0, j)),
            pl.BlockSpec((ng, d), lambda i, j: (0, 0)),
        ],
        out_specs=(pl.BlockSpec((tm, tn), lambda i, j: (i, j)),
                   pl.BlockSpec((ng, tm), lambda i, j: (0, i)),
                   pl.BlockSpec((tm, LANES), lambda i, j: (i, 0))),
        scratch_shapes=[pltpu.VMEM((tm, d), BF16)],
        compiler_params=_params("parallel", "arbitrary"),
        name="mlstm_in_proj",
    )(x, gain, w, wt)


def _fox_in_body(x_ref, g_ref, w_ref, wt_ref, gq_ref, gk_ref, q_ref, kf_ref, kb_ref, vf_ref, vb_ref,
                 og_ref, rows_ref, xn_ref, *, seg, hd, q_scale):
    j = pl.program_id(1)

    @pl.when(j == 0)
    def _():
        _norm_and_gates(x_ref, g_ref, wt_ref, xn_ref, rows_ref, None)

    z = _dot(xn_ref[...], w_ref[...])
    heads = [slice(c * hd, (c + 1) * hd) for c in range(z.shape[1] // hd)]

    @pl.when(j < seg)
    def _():
        for sl in heads:
            q_ref[:, sl] = (_rms(z[:, sl], gq_ref[...]) * q_scale).astype(BF16)

    @pl.when(jnp.logical_and(j >= seg, j < 2 * seg))
    def _():
        for sl in heads:
            y = _rms(z[:, sl], gk_ref[...])
            kf_ref[:, sl] = y
            kb_ref[:, sl] = y.astype(BF16)

    @pl.when(jnp.logical_and(j >= 2 * seg, j < 3 * seg))
    def _():
        vf_ref[...] = z
        vb_ref[...] = z.astype(BF16)

    @pl.when(j >= 3 * seg)
    def _():
        og_ref[...] = z


def _fox_in_proj(x, gain, w, wt, g_qk, layer, hd, q_scale):
    m, d = x.shape
    ng = wt.shape[0]
    tm = _tile(m, 1024)
    tn = _tile(d, 512)
    seg = d // tn
    seg_spec = lambda s: pl.BlockSpec((tm, tn), lambda i, j: (i, jnp.clip(j - s * seg, 0, seg - 1)))
    f32_out = jax.ShapeDtypeStruct((m, d), F32)
    bf16_out = jax.ShapeDtypeStruct((m, d), BF16)
    return pl.pallas_call(
        functools.partial(_fox_in_body, seg=seg, hd=hd, q_scale=q_scale),
        out_shape=(bf16_out, f32_out, bf16_out, f32_out, bf16_out, f32_out,
                   jax.ShapeDtypeStruct((ng, m), F32)),
        grid=(m // tm, 4 * seg),
        in_specs=[
            pl.BlockSpec((tm, d), lambda i, j: (i, 0), pipeline_mode=pl.Buffered(1)),
            pl.BlockSpec((1, d), lambda i, j: (0, 0)),
            pl.BlockSpec((None, d, tn), lambda i, j: (layer, 0, j)),
            pl.BlockSpec((ng, d), lambda i, j: (0, 0)),
            pl.BlockSpec((None, 1, hd), lambda i, j: (0, 0, 0)),
            pl.BlockSpec((None, 1, hd), lambda i, j: (1, 0, 0)),
        ],
        out_specs=(seg_spec(0), seg_spec(1), seg_spec(1), seg_spec(2), seg_spec(2), seg_spec(3),
                   pl.BlockSpec((ng, tm), lambda i, j: (0, i))),
        scratch_shapes=[pltpu.VMEM((tm, d), BF16)],
        compiler_params=_params("parallel", "arbitrary"),
        name="fox_in_proj",
    )(x, gain, w, wt, g_qk, g_qk)


def _out_body(a_ref, w_ref, x_ref, o_ref):
    o_ref[...] = x_ref[...] + _dot(a_ref[...], w_ref[...])


def _out_proj(a, w, x, layer):
    m, k = a.shape
    d = x.shape[1]
    tm = _tile(m, 1024)
    tn = _tile(d, 1024)
    return pl.pallas_call(
        _out_body,
        out_shape=jax.ShapeDtypeStruct((m, d), F32),
        grid=(m // tm, d // tn),
        in_specs=[
            pl.BlockSpec((tm, k), lambda i, j: (i, 0)),
            pl.BlockSpec((None, k, tn), lambda i, j: (layer, 0, j)),
            pl.BlockSpec((tm, tn), lambda i, j: (i, j)),
        ],
        out_specs=pl.BlockSpec((tm, tn), lambda i, j: (i, j)),
        compiler_params=_params("parallel", "arbitrary"),
        name="out_proj",
    )(a, w, x)


def _ple_body(x_ref, g_ref, wg_ref, p_ref, wp_ref, o_ref, *, col_chunk):
    x = x_ref[...]
    xn = _rms(x, g_ref[...]).astype(BF16)
    pb = p_ref[...].astype(BF16)
    for c in range(x.shape[1] // col_chunk):
        sl = slice(c * col_chunk, (c + 1) * col_chunk)
        gate = _sigmoid(_dot(xn, wg_ref[:, sl]))
        o_ref[:, sl] = x[:, sl] + gate * _dot(pb, wp_ref[:, sl])


def _ple(x, gain, w_gate, p, w_proj, layer):
    m, d = x.shape
    pd = p.shape[-1]
    tm = _tile(m, 512)
    return pl.pallas_call(
        functools.partial(_ple_body, col_chunk=_tile(d, 512)),
        out_shape=jax.ShapeDtypeStruct((m, d), F32),
        grid=(m // tm,),
        in_specs=[
            pl.BlockSpec((tm, d), lambda i: (i, 0)),
            pl.BlockSpec((1, d), lambda i: (0, 0)),
            pl.BlockSpec((None, d, d), lambda i: (layer, 0, 0), pipeline_mode=pl.Buffered(1)),
            pl.BlockSpec((None, tm, pd), lambda i: (layer, i, 0)),
            pl.BlockSpec((None, pd, d), lambda i: (layer, 0, 0), pipeline_mode=pl.Buffered(1)),
        ],
        out_specs=pl.BlockSpec((tm, d), lambda i: (i, 0)),
        compiler_params=_params("parallel"),
        name="ple",
    )(x, gain, w_gate, p, w_proj)


def _mlstm_body(q_ref, k_ref, v_ref, o_ref, gc_ref, gr_ref, bc_ref, br_ref, gh_ref,
                c0_ref, n0_ref, m0_ref, a_ref, c_ref, n_ref, m_ref, *, heads, dqk, dv):
    ci = pl.program_id(1)
    L = q_ref.shape[0]

    @pl.when(ci == 0)
    def _():
        c_ref[...] = c0_ref[...]
        n_ref[...] = n0_ref[...]
        m_ref[...] = m0_ref[...]

    row = lax.broadcasted_iota(jnp.int32, (L, L), 0)
    col = lax.broadcasted_iota(jnp.int32, (L, L), 1)
    causal = row >= col
    tril = causal.astype(F32)
    triu = (row <= col).astype(F32)

    gc = gc_ref[...] + bc_ref[...]
    gr = gr_ref[...] + br_ref[...]
    b_col = _dot_f32(tril, _log_sigmoid(gc))
    b_row = _dot_f32(_log_sigmoid(gr), triu)
    scale = dqk ** -0.5

    for h in range(heads):
        q = q_ref[:, h * dqk:(h + 1) * dqk]
        k = k_ref[:, h * dqk:(h + 1) * dqk] * scale
        v = v_ref[:, h * dv:(h + 1) * dv].astype(BF16)
        qb = q.astype(BF16)
        cst = c_ref[h]
        nst = n_ref[h]
        m_prev = m_ref[h][:, :1]

        bc = b_col[:, heads + h:heads + h + 1]
        ic = gc[:, h:h + 1]
        brow = b_row[heads + h:heads + h + 1, :]
        irow = gr[h:h + 1, :]

        dmat = jnp.where(causal, bc - brow + irow, NEG_INF)
        g = bc + m_prev
        m_t = jnp.maximum(g, jnp.max(dmat, axis=-1, keepdims=True))
        s = _dot_nt(qb, k.astype(BF16)) * jnp.exp(dmat - m_t)
        inter = jnp.exp(g - m_t)
        num = _dot(s.astype(BF16), v) + inter * _dot(qb, cst.astype(BF16))
        den = jnp.sum(s, axis=-1, keepdims=True) + inter * jnp.sum(q * nst, axis=-1, keepdims=True)
        hval = num / jnp.maximum(jnp.abs(den), jnp.exp(-m_t))

        hn = _rms(hval, gh_ref[:, h * dv:(h + 1) * dv])
        og = o_ref[:, h * dv:(h + 1) * dv]
        a_ref[:, h * dv:(h + 1) * dv] = (hn * _sigmoid(og)).astype(BF16)

        b_end = bc[L - 1:L, :]
        m_new = m_t[L - 1:L, :]
        decay = jnp.exp(b_end + m_prev - m_new)
        wa = jnp.exp(b_end - bc + ic - m_new)
        kw = k * wa
        c_ref[h] = decay * cst + _dot_tn(kw.astype(BF16), v)
        n_ref[h] = decay * nst + jnp.sum(kw, axis=0, keepdims=True)
        m_ref[h] = jnp.broadcast_to(m_new, (1, LANES))


def _mlstm(z, gates_col, gates_row, bias_lanes, bias_rows, g_h, c0, n0, m0, chunk):
    b, t, _ = z.shape
    heads, dqk, dv = c0.shape[1:]
    qk = heads * dqk
    vd = heads * dv
    assert (2 * qk) % vd == 0
    v_blk = 2 * qk // vd
    ng = gates_row.shape[1]
    nc = t // chunk
    st4 = lambda i, c: (i, 0, 0, 0)
    return pl.pallas_call(
        functools.partial(_mlstm_body, heads=heads, dqk=dqk, dv=dv),
        out_shape=(
            jax.ShapeDtypeStruct((b, t, vd), BF16),
            jax.ShapeDtypeStruct((b, heads, dqk, dv), F32),
            jax.ShapeDtypeStruct((b, heads, 1, dqk), F32),
            jax.ShapeDtypeStruct((b, heads, 1, LANES), F32),
        ),
        grid=(b, nc),
        in_specs=[
            pl.BlockSpec((None, chunk, qk), lambda i, c: (i, c, 0)),
            pl.BlockSpec((None, chunk, qk), lambda i, c: (i, c, 1)),
            pl.BlockSpec((None, chunk, vd), lambda i, c: (i, c, v_blk)),
            pl.BlockSpec((None, chunk, vd), lambda i, c: (i, c, v_blk + 1)),
            pl.BlockSpec((None, chunk, LANES), lambda i, c: (i, c, 0)),
            pl.BlockSpec((None, ng, chunk), lambda i, c: (i, 0, c)),
            pl.BlockSpec((1, LANES), lambda i, c: (0, 0)),
            pl.BlockSpec((ng, 1), lambda i, c: (0, 0)),
            pl.BlockSpec((1, vd), lambda i, c: (0, 0)),
            pl.BlockSpec((None, heads, dqk, dv), st4),
            pl.BlockSpec((None, heads, 1, dqk), st4),
            pl.BlockSpec((None, heads, 1, LANES), st4),
        ],
        out_specs=(
            pl.BlockSpec((None, chunk, vd), lambda i, c: (i, c, 0)),
            pl.BlockSpec((None, heads, dqk, dv), st4),
            pl.BlockSpec((None, heads, 1, dqk), st4),
            pl.BlockSpec((None, heads, 1, LANES), st4),
        ),
        compiler_params=_params("parallel", "arbitrary"),
        name="mlstm",
    )(z, z, z, z, gates_col, gates_row, bias_lanes, bias_rows, g_h, c0, n0, m0)


def _cumsum_body(f_ref, b_ref, lf_ref, c_ref, *col_refs, blk, activate, anchor_end):
    t = f_ref.shape[1]
    row = lax.broadcasted_iota(jnp.int32, (blk, blk), 0)
    col = lax.broadcasted_iota(jnp.int32, (blk, blk), 1)
    triu = (row <= col).astype(F32)
    carry = jnp.zeros((f_ref.shape[0], 1), F32)
    for i in range(t // blk):
        sl = slice(i * blk, (i + 1) * blk)
        seg = f_ref[:, sl]
        if activate:
            seg = _log_sigmoid(seg + b_ref[...])
        lf_ref[:, sl] = seg
        cs = _dot_f32(seg, triu) + carry
        c_ref[:, sl] = cs
        carry = cs[:, blk - 1:blk]
        if col_refs:
            col_refs[0][sl, :] = _rows_to_cols(seg)
            col_refs[1][sl, :] = _rows_to_cols(cs)
    if anchor_end:
        c_ref[...] = c_ref[...] - carry


def _forget_cumsum(f_rows, bias_col, activate, anchor_end, emit_cols):
    b, ng, t = f_rows.shape
    blk = _tile(t, 512)
    row_spec = pl.BlockSpec((None, ng, t), lambda i: (i, 0, 0))
    row_shape = jax.ShapeDtypeStruct((b, ng, t), F32)
    out_shape, out_specs = (row_shape, row_shape), (row_spec, row_spec)
    if emit_cols:
        assert not anchor_end
        col_shape = jax.ShapeDtypeStruct((b, t, LANES), F32)
        col_spec = pl.BlockSpec((None, t, LANES), lambda i: (i, 0, 0))
        out_shape, out_specs = out_shape + (col_shape, col_shape), out_specs + (col_spec, col_spec)
    return pl.pallas_call(
        functools.partial(_cumsum_body, blk=blk, activate=activate, anchor_end=anchor_end),
        out_shape=out_shape,
        grid=(b,),
        in_specs=[row_spec, pl.BlockSpec((ng, 1), lambda i: (0, 0))],
        out_specs=out_specs,
        compiler_params=_params("parallel"),
        name="forget_cumsum",
    )(f_rows, bias_col)


def _split3(c):
    hi = c.astype(BF16).astype(F32)
    r = c - hi
    mid = r.astype(BF16).astype(F32)
    return hi, mid, r - mid


def _bias_lanes(c, query_side):
    hi, mid, lo = _split3(c if query_side else -c)
    lane = lax.broadcasted_iota(jnp.int32, (c.shape[0], LANES), 1)
    o = 0 if query_side else 3
    terms = jnp.where(lane == o, hi, jnp.where(lane == o + 1, mid, jnp.where(lane == o + 2, lo, 0.0)))
    ones = jnp.logical_and(lane >= 3 - o, lane < 6 - o)
    return jnp.where(ones, 1.0, terms).astype(BF16)


def _online_softmax_step(s, v, m, l, acc):
    m_new = jnp.maximum(m, jnp.max(s, axis=-1, keepdims=True))
    p = jnp.exp2(s - m_new)
    alpha = jnp.exp2(m - m_new)
    l = alpha * l + jnp.sum(p, axis=-1, keepdims=True)
    acc = alpha * acc + _dot(p.astype(BF16), v)
    return m_new, l, acc


def _bias_rows(c, rows):
    hi, mid, lo = _split3(c)
    sub = lax.broadcasted_iota(jnp.int32, (rows, c.shape[1]), 0)
    terms = jnp.where(sub == 0, hi, jnp.where(sub == 1, mid, jnp.where(sub == 2, lo, 0.0)))
    return jnp.where(jnp.logical_and(sub >= 3, sub < 6), 1.0, terms).astype(BF16)


def _fox_prompt_body(q_ref, k_ref, v_ref, og_ref, cc_ref, cr_ref, a_ref, qt_ref, ka_ref, vt_ref, *, tq,
                     kbatch):
    h = pl.program_id(1)
    t, hd = q_ref.shape
    nt = t // tq
    for r in range(nt):
        rows = slice(r * tq, (r + 1) * tq)
        cc = cc_ref[rows, :]
        lane = lax.broadcasted_iota(jnp.int32, cc.shape, 1)
        c_col = jnp.sum(jnp.where(lane == h, cc, 0.0), axis=-1, keepdims=True) * LOG2E
        ka_ref[rows, :hd] = k_ref[rows, :]
        ka_ref[rows, hd:] = _bias_lanes(c_col, False)
        qt_ref[:hd, rows] = jnp.transpose(q_ref[rows, :].astype(F32)).astype(BF16)
        qt_ref[hd:, rows] = _bias_rows(cr_ref[r:r + 1, :] * LOG2E, LANES)
        vt_ref[:, rows] = jnp.transpose(v_ref[rows, :].astype(F32)).astype(BF16)

    key = lax.broadcasted_iota(jnp.int32, (tq, tq), 0)
    qry = lax.broadcasted_iota(jnp.int32, (tq, tq), 1)
    causal = key <= qry

    def tile_off(kj):
        return kj * tq if isinstance(kj, int) else pl.multiple_of(kj * tq, tq)

    for qi in range(nt):
        cols = slice(qi * tq, (qi + 1) * tq)
        qt = qt_ref[:, cols]

        def steps(carry, tiles):
            m, l, acc = carry
            scores = [_dot(ka_ref[pl.ds(tile_off(kj), tq), :], qt) for kj, _ in tiles]
            for s, (kj, diagonal) in zip(scores, tiles):
                if diagonal:
                    s = jnp.where(causal, s, NEG_INF)
                m_new = jnp.maximum(m, jnp.max(s, axis=0, keepdims=True))
                p = jnp.exp2(s - m_new)
                alpha = jnp.exp2(m - m_new)
                l = alpha * l + jnp.sum(p, axis=0, keepdims=True)
                acc = alpha * acc + _dot(vt_ref[:, pl.ds(tile_off(kj), tq)], p.astype(BF16))
                m = m_new
            return m, l, acc

        carry = (jnp.full((1, tq), NEG_INF, F32), jnp.zeros((1, tq), F32), jnp.zeros((hd, tq), F32))
        nb = qi // kbatch
        carry = lax.fori_loop(
            0, nb, lambda i, c: steps(c, [(i * kbatch + u, False) for u in range(kbatch)]), carry)
        tail = [(kj, False) for kj in range(nb * kbatch, qi)] + [(qi, True)]
        _, l, acc = steps(carry, tail)
        o = jnp.transpose(acc / l)
        a_ref[cols, :] = (o * _sigmoid(og_ref[cols, :])).astype(BF16)


def _fox_prompt_attn(qb, kb, vb, og, c_col, c_row, heads):
    b, t, d = qb.shape
    hd = d // heads
    tq = _tile(t, 512)
    nt = t // tq
    c_row4 = c_row.reshape(b, c_row.shape[1], nt, tq)
    head_blk = pl.BlockSpec((None, t, hd), lambda i, h: (i, 0, h))
    return pl.pallas_call(
        functools.partial(_fox_prompt_body, tq=tq, kbatch=ATTN_KEY_BATCH),
        out_shape=jax.ShapeDtypeStruct((b, t, d), BF16),
        grid=(b, heads),
        in_specs=[
            head_blk, head_blk, head_blk, head_blk,
            pl.BlockSpec((None, t, LANES), lambda i, h: (i, 0, 0)),
            pl.BlockSpec((None, None, nt, tq), lambda i, h: (i, h, 0, 0)),
        ],
        out_specs=head_blk,
        scratch_shapes=[pltpu.VMEM((hd + LANES, t), BF16), pltpu.VMEM((t, hd + LANES), BF16),
                        pltpu.VMEM((hd, t), BF16)],
        compiler_params=_params("parallel", "arbitrary"),
        name="fox_prompt_attn",
    )(qb, kb, vb, og, c_col, c_row4)


def _key_bias_lanes(c3, heads):
    lane = lax.broadcasted_iota(jnp.int32, c3.shape, 1)
    hi, mid, lo = _split3(c3 * jnp.where(lane < 3 * heads, LOG2E, 1.0))
    pick = jnp.where(lane < heads, hi, jnp.where(lane < 2 * heads, mid, jnp.where(lane < 3 * heads, lo, hi)))
    return pick.astype(BF16)


def _fox_sample_body(q_ref, kp_ref, vp_ref, kn_ref, vn_ref, og_ref, cq_ref, c3p_ref, c3n_ref,
                     a_ref, wq_ref, m_ref, l_ref, acc_ref, *, heads, hd, hg):
    j = pl.program_id(1)
    nj = pl.num_programs(1)
    tq = q_ref.shape[0]
    groups = heads // hg
    tk = kp_ref.shape[0] // heads

    @pl.when(j == 0)
    def _():
        m_ref[...] = jnp.full(m_ref.shape, NEG_INF, F32)
        l_ref[...] = jnp.zeros(l_ref.shape, F32)
        acc_ref[...] = jnp.zeros(acc_ref.shape, F32)
        wq_ref[...] = jnp.zeros(wq_ref.shape, BF16)
        lane = lax.broadcasted_iota(jnp.int32, (tq, LANES), 1)
        for g in range(groups):
            for i in range(hg):
                h = g * hg + i
                rows = slice(i * tq, (i + 1) * tq)
                wq_ref[g, rows, i * hd:(i + 1) * hd] = q_ref[:, h * hd:(h + 1) * hd]
                hi, mid, lo = _split3(cq_ref[:, h:h + 1] * LOG2E)
                own = jnp.logical_or(lane == h, jnp.logical_or(lane == heads + h, lane == 2 * heads + h))
                cq3 = jnp.where(lane == 3 * heads, hi,
                                jnp.where(lane == 3 * heads + 1, mid,
                                          jnp.where(lane == 3 * heads + 2, lo, 0.0)))
                wq_ref[g, rows, hg * hd:] = jnp.where(own, -1.0, cq3).astype(BF16)

    def update(g, s, v):
        m, l, acc = _online_softmax_step(s, v, m_ref[g], l_ref[g], acc_ref[g])
        m_ref[g] = m
        l_ref[g] = l
        acc_ref[g] = acc

    kbias = _key_bias_lanes(c3p_ref[...], heads)
    for g in range(groups):
        head_rows = lambda i: pl.ds(g * hg + i, tk, stride=heads)
        ks = [kp_ref[head_rows(i), :].astype(BF16) for i in range(hg)]
        vs = [vp_ref[head_rows(i), :].astype(BF16) for i in range(hg)]
        s = _dot_nt(wq_ref[g], jnp.concatenate(ks + [kbias], axis=1))
        update(g, s, jnp.concatenate(vs, axis=1))

    @pl.when(j == nj - 1)
    def _():
        row = lax.broadcasted_iota(jnp.int32, (hg * tq, tq), 0)
        col = lax.broadcasted_iota(jnp.int32, (hg * tq, tq), 1)
        causal = row % tq >= col
        nbias = _key_bias_lanes(c3n_ref[...], heads)
        for g in range(groups):
            cols = slice(g * hg * hd, (g + 1) * hg * hd)
            s = _dot_nt(wq_ref[g], jnp.concatenate([kn_ref[:, cols], nbias], axis=1))
            update(g, jnp.where(causal, s, NEG_INF), vn_ref[:, cols])
            for i in range(hg):
                rows = slice(i * tq, (i + 1) * tq)
                sl = slice((g * hg + i) * hd, (g * hg + i + 1) * hd)
                o = acc_ref[g, rows, i * hd:(i + 1) * hd] / l_ref[g, rows, :]
                a_ref[:, sl] = (o * _sigmoid(og_ref[:, sl])).astype(BF16)


def _fox_sample_attn(qb, k_past, v_past, layer, kb, vb, og, cq_col, c3_past, c3_new):
    b, tq, d = qb.shape
    p, heads, hd = k_past.shape[2:]
    assert 3 * heads + 3 <= LANES
    hg = max(1, min(heads, 512 // hd))
    assert heads % hg == 0
    tk = _tile(p, 512)
    new_blk = pl.BlockSpec((None, tq, d), lambda i, j: (i, 0, 0))
    past_blk = pl.BlockSpec((None, None, tk * heads, hd), lambda i, j: (layer, i, j, 0))
    flat = lambda a: a.reshape(a.shape[:2] + (p * heads, hd))
    return pl.pallas_call(
        functools.partial(_fox_sample_body, heads=heads, hd=hd, hg=hg),
        out_shape=jax.ShapeDtypeStruct((b, tq, d), BF16),
        grid=(b, p // tk),
        in_specs=[
            new_blk, past_blk, past_blk, new_blk, new_blk, new_blk,
            pl.BlockSpec((None, tq, LANES), lambda i, j: (i, 0, 0)),
            pl.BlockSpec((None, tk, LANES), lambda i, j: (i, j, 0)),
            pl.BlockSpec((None, tq, LANES), lambda i, j: (i, 0, 0)),
        ],
        out_specs=new_blk,
        scratch_shapes=[
            pltpu.VMEM((heads // hg, hg * tq, hg * hd + LANES), BF16),
            pltpu.VMEM((heads // hg, hg * tq, 1), F32),
            pltpu.VMEM((heads // hg, hg * tq, 1), F32),
            pltpu.VMEM((heads // hg, hg * tq, hg * hd), F32),
        ],
        compiler_params=_params("parallel", "arbitrary"),
        name="fox_sample_attn",
    )(qb, flat(k_past), flat(v_past), kb, vb, og, cq_col, c3_past, c3_new)


def _round_up(n, k):
    return (n + k - 1) // k * k


def _gate_weight_rows(w, col0, n):
    wt = jnp.transpose(w[:, col0:col0 + n]).astype(BF16)
    return jnp.pad(wt, ((0, _round_up(n, SUBLANES) - n), (0, 0)))


def _pad_rows(a, rows):
    return jnp.pad(a, ((0, rows - a.shape[0]),) + ((0, 0),) * (a.ndim - 1))


def _col_layout(rows):
    c = jnp.swapaxes(rows, 1, 2)
    return jnp.pad(c, ((0, 0), (0, 0), (0, LANES - c.shape[-1])))


def _c3_layout(rows, heads):
    c = jnp.swapaxes(rows[:, :heads, :], 1, 2)
    c3 = jnp.concatenate([c, c, c, jnp.ones(c.shape[:2] + (3,), F32)], axis=-1)
    return jnp.pad(c3, ((0, 0), (0, 0), (0, LANES - c3.shape[-1])))


def _run_trunk(x, p, w, mlstm_init, fox_past):
    b, t, d = x.shape
    depth = w["norm_gains"].shape[0]
    m_heads, dqk, dv = w["m_heads"], w["dqk"], w["dv"]
    f_heads = w["f_heads"]
    hd = d // f_heads
    qk, vd = m_heads * dqk, m_heads * dv
    mtok = b * t
    xf = x.reshape(mtok, d)
    pf = p.reshape(depth, mtok, p.shape[-1])
    chunk = _tile(t, 256)
    m_states, f_rows = [], []
    for i in range(depth):
        gains = w["norm_gains"][i]
        gain = lambda r: gains[r].reshape(1, d)
        j = i // 2
        xf = _ffn(xf, gain(0), w["ffn1_in"], w["ffn1_out"], i)
        if i % 2 == 0:
            z, g_rows, g_cols = _mlstm_in_proj(xf, gain(1), w["mlstm_w_in"], w["mlstm_gate_rows"][j],
                                               j, 2 * qk + 2 * vd)
            ng = g_rows.shape[0]
            g_rows = jnp.swapaxes(g_rows.reshape(ng, b, t), 0, 1)
            if mlstm_init is None:
                c0 = jnp.zeros((b, m_heads, dqk, dv), F32)
                n0 = jnp.zeros((b, m_heads, dqk), F32)
                m0 = jnp.zeros((b, m_heads), F32)
            else:
                c0, n0, m0 = mlstm_init[0][j], mlstm_init[1][j], mlstm_init[2][j]
            bias = w["mlstm_b_gates"][j].reshape(2 * m_heads)
            a, c_new, n_new, m_new = _mlstm(
                z.reshape(b, t, -1), g_cols.reshape(b, t, LANES), g_rows,
                jnp.pad(bias, (0, LANES - 2 * m_heads)).reshape(1, LANES),
                _pad_rows(bias.reshape(-1, 1), ng),
                w["mlstm_g_h"][j].reshape(1, vd),
                c0, n0.reshape(b, m_heads, 1, dqk),
                jnp.broadcast_to(m0[:, :, None, None], (b, m_heads, 1, LANES)), chunk)
            m_states.append((c_new, n_new.reshape(b, m_heads, dqk), m_new[:, :, 0, 0]))
            xf = _out_proj(a.reshape(mtok, vd), w["mlstm_w_out"], xf, j)
        else:
            qb, kf, kb, vf, vb, og, f_pre = _fox_in_proj(
                xf, gain(1), w["fox_w_in"], w["fox_gate_rows"][j],
                w["fox_g_qk"][j].reshape(2, 1, hd), j, hd, hd ** -0.5 * LOG2E)
            ng = f_pre.shape[0]
            f_pre = jnp.swapaxes(f_pre.reshape(ng, b, t), 0, 1)
            bias_col = _pad_rows(w["fox_b_f"][j].reshape(-1, 1), ng)
            shp = (b, t, d)
            if fox_past is None:
                lf_row, c_row, lf_col, c_col = _forget_cumsum(f_pre, bias_col, True, False, True)
                a = _fox_prompt_attn(qb.reshape(shp), kb.reshape(shp), vb.reshape(shp),
                                     og.reshape(shp), c_col, c_row, f_heads)
                lf = lf_col[:, :, :f_heads]
            else:
                lf_row, c_row = _forget_cumsum(f_pre, bias_col, True, False, False)
                lf_past_row = jnp.swapaxes(fox_past[2][j].astype(F32), 1, 2)
                lf_past_row = jnp.pad(lf_past_row, ((0, 0), (0, ng - f_heads), (0, 0)))
                _, c_past_row = _forget_cumsum(lf_past_row, bias_col, False, True, False)
                a = _fox_sample_attn(qb.reshape(shp), fox_past[0], fox_past[1], j,
                                     kb.reshape(shp), vb.reshape(shp), og.reshape(shp),
                                     _col_layout(c_row), _c3_layout(c_past_row, f_heads),
                                     _c3_layout(c_row, f_heads))
                lf = jnp.swapaxes(lf_row[:, :f_heads, :], 1, 2)
            f_rows.append((kf.reshape(b, t, f_heads, hd), vf.reshape(b, t, f_heads, hd), lf))
            xf = _out_proj(a.reshape(mtok, d), w["fox_w_out"], xf, j)
        xf = _ffn(xf, gain(2), w["ffn2_in"], w["ffn2_out"], i)
        xf = _ple(xf, gain(3), w["ple_gate"], pf, w["ple_proj"], i)
    stack = lambda k: jnp.stack([s[k] for s in m_states])
    stack_f = lambda k: jnp.stack([r[k] for r in f_rows])
    return (xf.reshape(b, t, d), stack(0), stack(1), stack(2), stack_f(0), stack_f(1), stack_f(2))


def kernel(x_prompt, x_sample, p_prompt, p_sample, state_mlstm_C, state_mlstm_n, state_mlstm_m, cache_fox_k, cache_fox_v, cache_fox_lf, norm_gains, ffn1_in, ffn1_out, ffn2_in, ffn2_out, ple_gate, ple_proj, mlstm_w_in, mlstm_b_gates, mlstm_g_h, mlstm_w_out, fox_w_in, fox_b_f, fox_g_qk, fox_w_out):
    d = x_prompt.shape[-1]
    m_heads, dqk, dv = state_mlstm_C.shape[2:]
    f_heads = cache_fox_lf.shape[-1]
    qk, vd = m_heads * dqk, m_heads * dv
    n_mlstm, n_fox = mlstm_w_in.shape[0], fox_w_in.shape[0]
    w = {
        "m_heads": m_heads, "dqk": dqk, "dv": dv, "f_heads": f_heads,
        "norm_gains": norm_gains, "mlstm_b_gates": mlstm_b_gates, "mlstm_g_h": mlstm_g_h,
        "fox_b_f": fox_b_f, "fox_g_qk": fox_g_qk,
        "ffn1_in": ffn1_in.astype(BF16), "ffn1_out": ffn1_out.astype(BF16),
        "ffn2_in": ffn2_in.astype(BF16), "ffn2_out": ffn2_out.astype(BF16),
        "ple_gate": ple_gate.astype(BF16), "ple_proj": ple_proj.astype(BF16),
        "mlstm_w_in": mlstm_w_in.astype(BF16), "mlstm_w_out": mlstm_w_out.astype(BF16),
        "fox_w_in": fox_w_in.astype(BF16), "fox_w_out": fox_w_out.astype(BF16),
        "mlstm_gate_rows": [_gate_weight_rows(mlstm_w_in[j], 2 * qk + 2 * vd, 2 * m_heads)
                            for j in range(n_mlstm)],
        "fox_gate_rows": [_gate_weight_rows(fox_w_in[j], 4 * d, f_heads) for j in range(n_fox)],
    }
    y_p, p_c, p_n, p_m, p_k, p_v, p_lf = _run_trunk(x_prompt, p_prompt, w, None, None)
    y_s, s_c, s_n, s_m, s_k, s_v, s_lf = _run_trunk(
        x_sample, p_sample, w, (state_mlstm_C, state_mlstm_n, state_mlstm_m),
        (cache_fox_k, cache_fox_v, cache_fox_lf))
    return (y_p, y_s, p_c, p_n, p_m, p_k, p_v, p_lf, s_c, s_n, s_m, s_k, s_v, s_lf)
```

```python
import functools

import jax
import jax.numpy as jnp
from jax import lax
from jax.experimental import pallas as pl
from jax.experimental.pallas import tpu as pltpu

F32 = jnp.float32
BF16 = jnp.bfloat16
NORM_EPS = 1e-6
FFN_HALF = 0.5
V7X_VMEM_BYTES = 64 * 1024 * 1024
VMEM_LIMIT = V7X_VMEM_BYTES - 8 * 1024 * 1024
LANES = 128
SUBLANES = 8
NEG_INF = float("-inf")
LOG2E = 1.4426950408889634
ATTN_KEY_BATCH = 4


def _params(*sem):
    return pltpu.CompilerParams(dimension_semantics=sem, vmem_limit_bytes=VMEM_LIMIT)


def _tile(n, pref):
    if n <= pref:
        return n
    for t in range(pref - pref % LANES, 0, -LANES):
        if n % t == 0:
            return t
    raise ValueError((n, pref))


def _round_up(n, k):
    return (n + k - 1) // k * k


def _rms(x, g):
    ms = jnp.mean(x * x, axis=-1, keepdims=True)
    return x * lax.rsqrt(ms + NORM_EPS) * g


def _sigmoid(x):
    return 1.0 / (1.0 + jnp.exp(-x))


def _log_sigmoid(x):
    return jnp.minimum(x, 0.0) - jnp.log(1.0 + jnp.exp(-jnp.abs(x)))


def _dot(a, b):
    return jnp.dot(a, b, preferred_element_type=F32)


def _dot_nt(a, b):
    return lax.dot_general(a, b, (((1,), (1,)), ((), ())), preferred_element_type=F32)


def _dot_tn(a, b):
    return lax.dot_general(a, b, (((0,), (0,)), ((), ())), preferred_element_type=F32)


def _dot_f32(a, b):
    return jnp.dot(a, b, preferred_element_type=F32, precision=lax.Precision.HIGHEST)


def _ffn_body(x_ref, g_ref, wg_ref, wu_ref, wo_ref, out_ref, *rest, col_chunk):
    xn_ref = rest[-1]
    f = pl.program_id(1)

    @pl.when(f == 0)
    def _():
        x = x_ref[...]
        xn_ref[...] = _rms(x, g_ref[...]).astype(BF16)
        out_ref[...] = x

    if len(rest) > 1:
        wg, wu, wo = (r[...].astype(BF16) for r in (wg_ref, wu_ref, wo_ref))
        for w, cast_ref in zip((wg, wu, wo), rest[:-1]):
            cast_ref[...] = w
        wo_cols = lambda sl: wo[:, sl]
    else:
        wg, wu = wg_ref[...], wu_ref[...]
        wo_cols = lambda sl: wo_ref[:, sl]
    xn = xn_ref[...]
    g = _dot(xn, wg)
    u = _dot(xn, wu)
    h = (g * _sigmoid(g) * u * FFN_HALF).astype(BF16)
    d = out_ref.shape[1]
    for c in range(d // col_chunk):
        sl = slice(c * col_chunk, (c + 1) * col_chunk)
        out_ref[:, sl] += _dot(h, wo_cols(sl))


def _ffn(x, gain, weights, layer):
    m, d = x.shape
    cast = len(weights) == 2
    f_dim = weights[1].shape[1] if cast else weights[2].shape[0]
    tm = _tile(m, 1024)
    tf = _tile(f_dim, 512)
    nf = f_dim // tf
    x_specs = [
        pl.BlockSpec((tm, d), lambda i, f: (i, 0), pipeline_mode=pl.Buffered(1)),
        pl.BlockSpec((1, d), lambda i, f: (0, 0)),
    ]
    col_blk = pl.BlockSpec((d, tf), lambda i, f: (0, f))
    row_blk = pl.BlockSpec((tf, d), lambda i, f: (f, 0))
    out_shape = jax.ShapeDtypeStruct((m, d), F32)
    out_spec = pl.BlockSpec((tm, d), lambda i, f: (i, 0))
    if cast:
        assert m == tm, "the cast outputs are written once per hidden tile"
        w_in, w_out = weights
        w_specs = [
            pl.BlockSpec((None, d, tf), lambda i, f: (layer, 0, f)),
            pl.BlockSpec((None, d, tf), lambda i, f: (layer, 0, f + nf)),
            pl.BlockSpec((None, tf, d), lambda i, f: (layer, f, 0)),
        ]
        args = (w_in, w_in, w_out)
        out_shape = (out_shape, jax.ShapeDtypeStruct((d, f_dim), BF16),
                     jax.ShapeDtypeStruct((d, f_dim), BF16), jax.ShapeDtypeStruct((f_dim, d), BF16))
        out_spec = (out_spec, col_blk, col_blk, row_blk)
    else:
        w_specs = [col_blk, col_blk, row_blk]
        args = tuple(weights)
    res = pl.pallas_call(
        functools.partial(_ffn_body, col_chunk=_tile(d, 512)),
        out_shape=out_shape,
        grid=(m // tm, nf),
        in_specs=x_specs + w_specs,
        out_specs=out_spec,
        scratch_shapes=[pltpu.VMEM((tm, d), BF16)],
        compiler_params=_params("parallel", "arbitrary"),
        name="ffn_cast" if cast else "ffn",
    )(x, gain, *args)
    return (res[0], tuple(res[1:])) if cast else res


def _rows_to_cols(rows):
    ng, n = rows.shape
    padded = jnp.concatenate([rows, jnp.zeros((LANES - ng, n), F32)], axis=0)
    return jnp.transpose(padded)


def _norm_and_gates(x_ref, g_ref, wgate_ref, xn_ref, rows_ref, cols_ref, n_gates):
    xn = _rms(x_ref[...], g_ref[...]).astype(BF16)
    xn_ref[...] = xn
    cols = _dot(xn, wgate_ref[...])
    lane = lax.broadcasted_iota(jnp.int32, cols.shape, 1)
    cols = jnp.where(lane < n_gates, cols, 0.0)
    rows_ref[...] = jnp.transpose(cols)[:rows_ref.shape[0], :]
    if cols_ref is not None:
        cols_ref[...] = cols


def _mlstm_in_body(x_ref, g_ref, w_ref, wgate_ref, z_ref, rows_ref, cols_ref, xn_ref, *, n_gates):
    @pl.when(pl.program_id(1) == 0)
    def _():
        _norm_and_gates(x_ref, g_ref, wgate_ref, xn_ref, rows_ref, cols_ref, n_gates)

    z_ref[...] = _dot(xn_ref[...], w_ref[...])


def _mlstm_in_proj(x, gain, w, layer, ncols, n_gates):
    m, d = x.shape
    ng = _round_up(n_gates, SUBLANES)
    tm = _tile(m, 1024)
    tn = _tile(ncols, 1024)
    assert ncols % LANES == 0 and n_gates <= LANES
    return pl.pallas_call(
        functools.partial(_mlstm_in_body, n_gates=n_gates),
        out_shape=(jax.ShapeDtypeStruct((m, ncols), F32), jax.ShapeDtypeStruct((ng, m), F32),
                   jax.ShapeDtypeStruct((m, LANES), F32)),
        grid=(m // tm, ncols // tn),
        in_specs=[
            pl.BlockSpec((tm, d), lambda i, j: (i, 0), pipeline_mode=pl.Buffered(1)),
            pl.BlockSpec((1, d), lambda i, j: (0, 0)),
            pl.BlockSpec((None, d, tn), lambda i, j: (layer, 0, j)),
            pl.BlockSpec((None, d, LANES), lambda i, j: (layer, 0, ncols // LANES)),
        ],
        out_specs=(pl.BlockSpec((tm, tn), lambda i, j: (i, j)),
                   pl.BlockSpec((ng, tm), lambda i, j: (0, i)),
                   pl.BlockSpec((tm, LANES), lambda i, j: (i, 0))),
        scratch_shapes=[pltpu.VMEM((tm, d), BF16)],
        compiler_params=_params("parallel", "arbitrary"),
        name="mlstm_in_proj",
    )(x, gain, w, w)


def _fox_in_body(x_ref, g_ref, w_ref, wgate_ref, gq_ref, gk_ref, q_ref, kf_ref, kb_ref, vf_ref, vb_ref,
                 og_ref, rows_ref, xn_ref, *, seg, hd, q_scale, n_gates):
    j = pl.program_id(1)

    @pl.when(j == 0)
    def _():
        _norm_and_gates(x_ref, g_ref, wgate_ref, xn_ref, rows_ref, None, n_gates)

    z = _dot(xn_ref[...], w_ref[...])
    heads = [slice(c * hd, (c + 1) * hd) for c in range(z.shape[1] // hd)]

    @pl.when(j < seg)
    def _():
        for sl in heads:
            q_ref[:, sl] = (_rms(z[:, sl], gq_ref[...]) * q_scale).astype(BF16)

    @pl.when(jnp.logical_and(j >= seg, j < 2 * seg))
    def _():
        for sl in heads:
            y = _rms(z[:, sl], gk_ref[...])
            kf_ref[:, sl] = y
            kb_ref[:, sl] = y.astype(BF16)

    @pl.when(jnp.logical_and(j >= 2 * seg, j < 3 * seg))
    def _():
        vf_ref[...] = z
        vb_ref[...] = z.astype(BF16)

    @pl.when(j >= 3 * seg)
    def _():
        og_ref[...] = z


def _fox_in_proj(x, gain, w, g_qk, layer, hd, q_scale):
    m, d = x.shape
    n_gates = d // hd
    ng = _round_up(n_gates, SUBLANES)
    tm = _tile(m, 1024)
    tn = _tile(d, 512)
    seg = d // tn
    seg_spec = lambda s: pl.BlockSpec((tm, tn), lambda i, j: (i, jnp.clip(j - s * seg, 0, seg - 1)))
    f32_out = jax.ShapeDtypeStruct((m, d), F32)
    bf16_out = jax.ShapeDtypeStruct((m, d), BF16)
    return pl.pallas_call(
        functools.partial(_fox_in_body, seg=seg, hd=hd, q_scale=q_scale, n_gates=n_gates),
        out_shape=(bf16_out, f32_out, bf16_out, f32_out, bf16_out, f32_out,
                   jax.ShapeDtypeStruct((ng, m), F32)),
        grid=(m // tm, 4 * seg),
        in_specs=[
            pl.BlockSpec((tm, d), lambda i, j: (i, 0), pipeline_mode=pl.Buffered(1)),
            pl.BlockSpec((1, d), lambda i, j: (0, 0)),
            pl.BlockSpec((None, d, tn), lambda i, j: (layer, 0, j)),
            pl.BlockSpec((None, d, LANES), lambda i, j: (layer, 0, 4 * d // LANES)),
            pl.BlockSpec((None, 1, hd), lambda i, j: (0, 0, 0)),
            pl.BlockSpec((None, 1, hd), lambda i, j: (1, 0, 0)),
        ],
        out_specs=(seg_spec(0), seg_spec(1), seg_spec(1), seg_spec(2), seg_spec(2), seg_spec(3),
                   pl.BlockSpec((ng, tm), lambda i, j: (0, i))),
        scratch_shapes=[pltpu.VMEM((tm, d), BF16)],
        compiler_params=_params("parallel", "arbitrary"),
        name="fox_in_proj",
    )(x, gain, w, w, g_qk, g_qk)


def _out_body(a_ref, w_ref, x_ref, o_ref):
    o_ref[...] = x_ref[...] + _dot(a_ref[...], w_ref[...])


def _out_proj(a, w, x, layer):
    m, k = a.shape
    d = x.shape[1]
    tm = _tile(m, 1024)
    tn = _tile(d, 1024)
    return pl.pallas_call(
        _out_body,
        out_shape=jax.ShapeDtypeStruct((m, d), F32),
        grid=(m // tm, d // tn),
        in_specs=[
            pl.BlockSpec((tm, k), lambda i, j: (i, 0)),
            pl.BlockSpec((None, k, tn), lambda i, j: (layer, 0, j)),
            pl.BlockSpec((tm, tn), lambda i, j: (i, j)),
        ],
        out_specs=pl.BlockSpec((tm, tn), lambda i, j: (i, j)),
        compiler_params=_params("parallel", "arbitrary"),
        name="out_proj",
    )(a, w, x)


def _ple_body(x_ref, g_ref, wg_ref, p_ref, wp_ref, o_ref, *, col_chunk):
    x = x_ref[...]
    xn = _rms(x, g_ref[...]).astype(BF16)
    pb = p_ref[...].astype(BF16)
    for c in range(x.shape[1] // col_chunk):
        sl = slice(c * col_chunk, (c + 1) * col_chunk)
        gate = _sigmoid(_dot(xn, wg_ref[:, sl]))
        o_ref[:, sl] = x[:, sl] + gate * _dot(pb, wp_ref[:, sl])


def _ple(x, gain, w_gate, p, w_proj, layer):
    m, d = x.shape
    pd = p.shape[-1]
    tm = _tile(m, 512)
    return pl.pallas_call(
        functools.partial(_ple_body, col_chunk=_tile(d, 512)),
        out_shape=jax.ShapeDtypeStruct((m, d), F32),
        grid=(m // tm,),
        in_specs=[
            pl.BlockSpec((tm, d), lambda i: (i, 0)),
            pl.BlockSpec((1, d), lambda i: (0, 0)),
            pl.BlockSpec((None, d, d), lambda i: (layer, 0, 0), pipeline_mode=pl.Buffered(1)),
            pl.BlockSpec((None, tm, pd), lambda i: (layer, i, 0)),
            pl.BlockSpec((None, pd, d), lambda i: (layer, 0, 0), pipeline_mode=pl.Buffered(1)),
        ],
        out_specs=pl.BlockSpec((tm, d), lambda i: (i, 0)),
        compiler_params=_params("parallel"),
        name="ple",
    )(x, gain, w_gate, p, w_proj)


def _mlstm_body(q_ref, k_ref, v_ref, o_ref, gc_ref, gr_ref, bc_ref, br_ref, gh_ref,
                c0_ref, n0_ref, m0_ref, a_ref, c_ref, n_ref, m_ref, *, heads, dqk, dv):
    ci = pl.program_id(1)
    L = q_ref.shape[0]

    @pl.when(ci == 0)
    def _():
        c_ref[...] = c0_ref[...]
        n_ref[...] = n0_ref[...]
        m_ref[...] = m0_ref[...]

    row = lax.broadcasted_iota(jnp.int32, (L, L), 0)
    col = lax.broadcasted_iota(jnp.int32, (L, L), 1)
    causal = row >= col
    tril = causal.astype(F32)
    triu = (row <= col).astype(F32)

    gc = gc_ref[...] + bc_ref[...]
    gr = gr_ref[...] + br_ref[...]
    b_col = _dot_f32(tril, _log_sigmoid(gc))
    b_row = _dot_f32(_log_sigmoid(gr), triu)
    scale = dqk ** -0.5

    for h in range(heads):
        q = q_ref[:, h * dqk:(h + 1) * dqk]
        k = k_ref[:, h * dqk:(h + 1) * dqk] * scale
        v = v_ref[:, h * dv:(h + 1) * dv].astype(BF16)
        qb = q.astype(BF16)
        cst = c_ref[h]
        nst = n_ref[h]
        m_prev = m_ref[h][:, :1]

        bc = b_col[:, heads + h:heads + h + 1]
        ic = gc[:, h:h + 1]
        brow = b_row[heads + h:heads + h + 1, :]
        irow = gr[h:h + 1, :]

        dmat = jnp.where(causal, bc - brow + irow, NEG_INF)
        g = bc + m_prev
        m_t = jnp.maximum(g, jnp.max(dmat, axis=-1, keepdims=True))
        s = _dot_nt(qb, k.astype(BF16)) * jnp.exp(dmat - m_t)
        inter = jnp.exp(g - m_t)
        num = _dot(s.astype(BF16), v) + inter * _dot(qb, cst.astype(BF16))
        den = jnp.sum(s, axis=-1, keepdims=True) + inter * jnp.sum(q * nst, axis=-1, keepdims=True)
        hval = num / jnp.maximum(jnp.abs(den), jnp.exp(-m_t))

        hn = _rms(hval, gh_ref[:, h * dv:(h + 1) * dv])
        og = o_ref[:, h * dv:(h + 1) * dv]
        a_ref[:, h * dv:(h + 1) * dv] = (hn * _sigmoid(og)).astype(BF16)

        b_end = bc[L - 1:L, :]
        m_new = m_t[L - 1:L, :]
        decay = jnp.exp(b_end + m_prev - m_new)
        wa = jnp.exp(b_end - bc + ic - m_new)
        kw = k * wa
        c_ref[h] = decay * cst + _dot_tn(kw.astype(BF16), v)
        n_ref[h] = decay * nst + jnp.sum(kw, axis=0, keepdims=True)
        m_ref[h] = jnp.broadcast_to(m_new, (1, LANES))


def _mlstm(z, gates_col, gates_row, bias_lanes, bias_rows, g_h, c0, n0, m0, chunk):
    b, t, _ = z.shape
    heads, dqk, dv = c0.shape[1:]
    qk = heads * dqk
    vd = heads * dv
    assert (2 * qk) % vd == 0
    v_blk = 2 * qk // vd
    ng = gates_row.shape[1]
    nc = t // chunk
    st4 = lambda i, c: (i, 0, 0, 0)
    return pl.pallas_call(
        functools.partial(_mlstm_body, heads=heads, dqk=dqk, dv=dv),
        out_shape=(
            jax.ShapeDtypeStruct((b, t, vd), BF16),
            jax.ShapeDtypeStruct((b, heads, dqk, dv), F32),
            jax.ShapeDtypeStruct((b, heads, 1, dqk), F32),
            jax.ShapeDtypeStruct((b, heads, 1, LANES), F32),
        ),
        grid=(b, nc),
        in_specs=[
            pl.BlockSpec((None, chunk, qk), lambda i, c: (i, c, 0)),
            pl.BlockSpec((None, chunk, qk), lambda i, c: (i, c, 1)),
            pl.BlockSpec((None, chunk, vd), lambda i, c: (i, c, v_blk)),
            pl.BlockSpec((None, chunk, vd), lambda i, c: (i, c, v_blk + 1)),
            pl.BlockSpec((None, chunk, LANES), lambda i, c: (i, c, 0)),
            pl.BlockSpec((None, ng, chunk), lambda i, c: (i, 0, c)),
            pl.BlockSpec((1, LANES), lambda i, c: (0, 0)),
            pl.BlockSpec((ng, 1), lambda i, c: (0, 0)),
            pl.BlockSpec((1, vd), lambda i, c: (0, 0)),
            pl.BlockSpec((None, heads, dqk, dv), st4),
            pl.BlockSpec((None, heads, 1, dqk), st4),
            pl.BlockSpec((None, heads, 1, LANES), st4),
        ],
        out_specs=(
            pl.BlockSpec((None, chunk, vd), lambda i, c: (i, c, 0)),
            pl.BlockSpec((None, heads, dqk, dv), st4),
            pl.BlockSpec((None, heads, 1, dqk), st4),
            pl.BlockSpec((None, heads, 1, LANES), st4),
        ),
        compiler_params=_params("parallel", "arbitrary"),
        name="mlstm",
    )(z, z, z, z, gates_col, gates_row, bias_lanes, bias_rows, g_h, c0, n0, m0)


def _cumsum_body(f_ref, b_ref, lf_ref, c_ref, *col_refs, blk, activate, anchor_end, heads):
    t = f_ref.shape[1]
    row = lax.broadcasted_iota(jnp.int32, (blk, blk), 0)
    col = lax.broadcasted_iota(jnp.int32, (blk, blk), 1)
    triu = (row <= col).astype(F32)
    carry = jnp.zeros((f_ref.shape[0], 1), F32)
    for i in range(t // blk):
        sl = slice(i * blk, (i + 1) * blk)
        seg = f_ref[:, sl]
        if activate:
            seg = _log_sigmoid(seg + b_ref[...])
        lf_ref[:, sl] = seg
        cs = _dot_f32(seg, triu) + carry
        c_ref[:, sl] = cs
        carry = cs[:, blk - 1:blk]
        if col_refs and not anchor_end:
            col_refs[0][sl, :] = _rows_to_cols(seg)
            col_refs[1][sl, :] = _rows_to_cols(cs)
    if anchor_end:
        c_ref[...] = c_ref[...] - carry
        if col_refs:
            sub = lax.broadcasted_iota(jnp.int32, (SUBLANES, blk), 0)
            ones3 = jnp.where(sub < 3, 1.0, 0.0)
            for i in range(t // blk):
                sl = slice(i * blk, (i + 1) * blk)
                c = c_ref[:heads, sl]
                col_refs[0][sl, :] = _rows_to_cols(jnp.concatenate([c, c, c, ones3], axis=0))


def _forget_cumsum(f_rows, bias_col, activate, anchor_end, emit_cols, heads=None):
    b, ng, t = f_rows.shape
    blk = _tile(t, 512)
    row_spec = pl.BlockSpec((None, ng, t), lambda i: (i, 0, 0))
    row_shape = jax.ShapeDtypeStruct((b, ng, t), F32)
    out_shape, out_specs = (row_shape, row_shape), (row_spec, row_spec)
    if emit_cols:
        col_shape = jax.ShapeDtypeStruct((b, t, LANES), F32)
        col_spec = pl.BlockSpec((None, t, LANES), lambda i: (i, 0, 0))
        n_cols = 1 if anchor_end else 2
        out_shape, out_specs = out_shape + (col_shape,) * n_cols, out_specs + (col_spec,) * n_cols
    return pl.pallas_call(
        functools.partial(_cumsum_body, blk=blk, activate=activate, anchor_end=anchor_end, heads=heads),
        out_shape=out_shape,
        grid=(b,),
        in_specs=[row_spec, pl.BlockSpec((ng, 1), lambda i: (0, 0))],
        out_specs=out_specs,
        compiler_params=_params("parallel"),
        name="forget_cumsum",
    )(f_rows, bias_col)


def _split3(c):
    hi = c.astype(BF16).astype(F32)
    r = c - hi
    mid = r.astype(BF16).astype(F32)
    return hi, mid, r - mid


def _bias_lanes(c, query_side):
    hi, mid, lo = _split3(c if query_side else -c)
    lane = lax.broadcasted_iota(jnp.int32, (c.shape[0], LANES), 1)
    o = 0 if query_side else 3
    terms = jnp.where(lane == o, hi, jnp.where(lane == o + 1, mid, jnp.where(lane == o + 2, lo, 0.0)))
    ones = jnp.logical_and(lane >= 3 - o, lane < 6 - o)
    return jnp.where(ones, 1.0, terms).astype(BF16)


def _online_softmax_step(s, v, m, l, acc):
    m_new = jnp.maximum(m, jnp.max(s, axis=-1, keepdims=True))
    p = jnp.exp2(s - m_new)
    alpha = jnp.exp2(m - m_new)
    l = alpha * l + jnp.sum(p, axis=-1, keepdims=True)
    acc = alpha * acc + _dot(p.astype(BF16), v)
    return m_new, l, acc


def _bias_rows(c, rows):
    hi, mid, lo = _split3(c)
    sub = lax.broadcasted_iota(jnp.int32, (rows, c.shape[1]), 0)
    terms = jnp.where(sub == 0, hi, jnp.where(sub == 1, mid, jnp.where(sub == 2, lo, 0.0)))
    return jnp.where(jnp.logical_and(sub >= 3, sub < 6), 1.0, terms).astype(BF16)


def _fox_prompt_body(q_ref, k_ref, v_ref, og_ref, cc_ref, cr_ref, a_ref, qt_ref, ka_ref, vt_ref, *, tq,
                     kbatch):
    h = pl.program_id(1)
    t, hd = q_ref.shape
    nt = t // tq
    for r in range(nt):
        rows = slice(r * tq, (r + 1) * tq)
        cc = cc_ref[rows, :]
        lane = lax.broadcasted_iota(jnp.int32, cc.shape, 1)
        c_col = jnp.sum(jnp.where(lane == h, cc, 0.0), axis=-1, keepdims=True) * LOG2E
        ka_ref[rows, :hd] = k_ref[rows, :]
        ka_ref[rows, hd:] = _bias_lanes(c_col, False)
        qt_ref[:hd, rows] = jnp.transpose(q_ref[rows, :].astype(F32)).astype(BF16)
        qt_ref[hd:, rows] = _bias_rows(cr_ref[r:r + 1, :] * LOG2E, LANES)
        vt_ref[:, rows] = jnp.transpose(v_ref[rows, :].astype(F32)).astype(BF16)

    key = lax.broadcasted_iota(jnp.int32, (tq, tq), 0)
    qry = lax.broadcasted_iota(jnp.int32, (tq, tq), 1)
    causal = key <= qry

    def tile_off(kj):
        return kj * tq if isinstance(kj, int) else pl.multiple_of(kj * tq, tq)

    for qi in range(nt):
        cols = slice(qi * tq, (qi + 1) * tq)
        qt = qt_ref[:, cols]

        def steps(carry, tiles):
            m, l, acc = carry
            scores = [_dot(ka_ref[pl.ds(tile_off(kj), tq), :], qt) for kj, _ in tiles]
            for s, (kj, diagonal) in zip(scores, tiles):
                if diagonal:
                    s = jnp.where(causal, s, NEG_INF)
                m_new = jnp.maximum(m, jnp.max(s, axis=0, keepdims=True))
                p = jnp.exp2(s - m_new)
                alpha = jnp.exp2(m - m_new)
                l = alpha * l + jnp.sum(p, axis=0, keepdims=True)
                acc = alpha * acc + _dot(vt_ref[:, pl.ds(tile_off(kj), tq)], p.astype(BF16))
                m = m_new
            return m, l, acc

        carry = (jnp.full((1, tq), NEG_INF, F32), jnp.zeros((1, tq), F32), jnp.zeros((hd, tq), F32))
        nb = qi // kbatch
        carry = lax.fori_loop(
            0, nb, lambda i, c: steps(c, [(i * kbatch + u, False) for u in range(kbatch)]), carry)
        tail = [(kj, False) for kj in range(nb * kbatch, qi)] + [(qi, True)]
        _, l, acc = steps(carry, tail)
        o = jnp.transpose(acc / l)
        a_ref[cols, :] = (o * _sigmoid(og_ref[cols, :])).astype(BF16)


def _fox_prompt_attn(qb, kb, vb, og, c_col, c_row, heads):
    b, t, d = qb.shape
    hd = d // heads
    tq = _tile(t, 512)
    nt = t // tq
    c_row4 = c_row.reshape(b, c_row.shape[1], nt, tq)
    head_blk = pl.BlockSpec((None, t, hd), lambda i, h: (i, 0, h))
    return pl.pallas_call(
        functools.partial(_fox_prompt_body, tq=tq, kbatch=ATTN_KEY_BATCH),
        out_shape=jax.ShapeDtypeStruct((b, t, d), BF16),
        grid=(b, heads),
        in_specs=[
            head_blk, head_blk, head_blk, head_blk,
            pl.BlockSpec((None, t, LANES), lambda i, h: (i, 0, 0)),
            pl.BlockSpec((None, None, nt, tq), lambda i, h: (i, h, 0, 0)),
        ],
        out_specs=head_blk,
        scratch_shapes=[pltpu.VMEM((hd + LANES, t), BF16), pltpu.VMEM((t, hd + LANES), BF16),
                        pltpu.VMEM((hd, t), BF16)],
        compiler_params=_params("parallel", "arbitrary"),
        name="fox_prompt_attn",
    )(qb, kb, vb, og, c_col, c_row4)


def _key_bias_lanes(c3, heads):
    lane = lax.broadcasted_iota(jnp.int32, c3.shape, 1)
    hi, mid, lo = _split3(c3 * jnp.where(lane < 3 * heads, LOG2E, 1.0))
    pick = jnp.where(lane < heads, hi, jnp.where(lane < 2 * heads, mid, jnp.where(lane < 3 * heads, lo, hi)))
    return pick.astype(BF16)


def _fox_sample_body(q_ref, kp_ref, vp_ref, kn_ref, vn_ref, og_ref, cq_ref, c3p_ref, c3n_ref,
                     a_ref, wq_ref, m_ref, l_ref, acc_ref, *, heads, hd, hg):
    j = pl.program_id(1)
    nj = pl.num_programs(1)
    tq = q_ref.shape[0]
    groups = heads // hg
    tk = kp_ref.shape[0] // heads

    @pl.when(j == 0)
    def _():
        m_ref[...] = jnp.full(m_ref.shape, NEG_INF, F32)
        l_ref[...] = jnp.zeros(l_ref.shape, F32)
        acc_ref[...] = jnp.zeros(acc_ref.shape, F32)
        wq_ref[...] = jnp.zeros(wq_ref.shape, BF16)
        lane = lax.broadcasted_iota(jnp.int32, (tq, LANES), 1)
        for g in range(groups):
            for i in range(hg):
                h = g * hg + i
                rows = slice(i * tq, (i + 1) * tq)
                wq_ref[g, rows, i * hd:(i + 1) * hd] = q_ref[:, h * hd:(h + 1) * hd]
                hi, mid, lo = _split3(cq_ref[:, h:h + 1] * LOG2E)
                own = jnp.logical_or(lane == h, jnp.logical_or(lane == heads + h, lane == 2 * heads + h))
                cq3 = jnp.where(lane == 3 * heads, hi,
                                jnp.where(lane == 3 * heads + 1, mid,
                                          jnp.where(lane == 3 * heads + 2, lo, 0.0)))
                wq_ref[g, rows, hg * hd:] = jnp.where(own, -1.0, cq3).astype(BF16)

    def update(g, s, v):
        m, l, acc = _online_softmax_step(s, v, m_ref[g], l_ref[g], acc_ref[g])
        m_ref[g] = m
        l_ref[g] = l
        acc_ref[g] = acc

    kbias = _key_bias_lanes(c3p_ref[...], heads)
    for g in range(groups):
        head_rows = lambda i: pl.ds(g * hg + i, tk, stride=heads)
        ks = [kp_ref[head_rows(i), :].astype(BF16) for i in range(hg)]
        vs = [vp_ref[head_rows(i), :].astype(BF16) for i in range(hg)]
        s = _dot_nt(wq_ref[g], jnp.concatenate(ks + [kbias], axis=1))
        update(g, s, jnp.concatenate(vs, axis=1))

    @pl.when(j == nj - 1)
    def _():
        row = lax.broadcasted_iota(jnp.int32, (hg * tq, tq), 0)
        col = lax.broadcasted_iota(jnp.int32, (hg * tq, tq), 1)
        causal = row % tq >= col
        nbias = _key_bias_lanes(c3n_ref[...], heads)
        for g in range(groups):
            cols = slice(g * hg * hd, (g + 1) * hg * hd)
            s = _dot_nt(wq_ref[g], jnp.concatenate([kn_ref[:, cols], nbias], axis=1))
            update(g, jnp.where(causal, s, NEG_INF), vn_ref[:, cols])
            for i in range(hg):
                rows = slice(i * tq, (i + 1) * tq)
                sl = slice((g * hg + i) * hd, (g * hg + i + 1) * hd)
                o = acc_ref[g, rows, i * hd:(i + 1) * hd] / l_ref[g, rows, :]
                a_ref[:, sl] = (o * _sigmoid(og_ref[:, sl])).astype(BF16)


def _fox_sample_attn(qb, k_past, v_past, layer, kb, vb, og, cq_col, c3_past, c3_new):
    b, tq, d = qb.shape
    p, heads, hd = k_past.shape[2:]
    assert 3 * heads + 3 <= LANES
    hg = max(1, min(heads, 512 // hd))
    assert heads % hg == 0
    tk = _tile(p, 1024)
    new_blk = pl.BlockSpec((None, tq, d), lambda i, j: (i, 0, 0))
    past_blk = pl.BlockSpec((None, None, tk * heads, hd), lambda i, j: (layer, i, j, 0))
    flat = lambda a: a.reshape(a.shape[:2] + (p * heads, hd))
    return pl.pallas_call(
        functools.partial(_fox_sample_body, heads=heads, hd=hd, hg=hg),
        out_shape=jax.ShapeDtypeStruct((b, tq, d), BF16),
        grid=(b, p // tk),
        in_specs=[
            new_blk, past_blk, past_blk, new_blk, new_blk, new_blk,
            pl.BlockSpec((None, tq, LANES), lambda i, j: (i, 0, 0)),
            pl.BlockSpec((None, tk, LANES), lambda i, j: (i, j, 0)),
            pl.BlockSpec((None, tq, LANES), lambda i, j: (i, 0, 0)),
        ],
        out_specs=new_blk,
        scratch_shapes=[
            pltpu.VMEM((heads // hg, hg * tq, hg * hd + LANES), BF16),
            pltpu.VMEM((heads // hg, hg * tq, 1), F32),
            pltpu.VMEM((heads // hg, hg * tq, 1), F32),
            pltpu.VMEM((heads // hg, hg * tq, hg * hd), F32),
        ],
        compiler_params=_params("parallel", "arbitrary"),
        name="fox_sample_attn",
    )(qb, flat(k_past), flat(v_past), kb, vb, og, cq_col, c3_past, c3_new)


def _pad_rows(a, rows):
    return jnp.pad(a, ((0, rows - a.shape[0]),) + ((0, 0),) * (a.ndim - 1))


def _col_layout(rows):
    c = jnp.swapaxes(rows, 1, 2)
    return jnp.pad(c, ((0, 0), (0, 0), (0, LANES - c.shape[-1])))


def _c3_layout(rows, heads):
    c = jnp.swapaxes(rows[:, :heads, :], 1, 2)
    c3 = jnp.concatenate([c, c, c, jnp.ones(c.shape[:2] + (3,), F32)], axis=-1)
    return jnp.pad(c3, ((0, 0), (0, 0), (0, LANES - c3.shape[-1])))


def _run_trunk(x, p, w, mlstm_init, fox_past):
    b, t, d = x.shape
    depth = w["norm_gains"].shape[0]
    m_heads, dqk, dv = w["m_heads"], w["dqk"], w["dv"]
    f_heads = w["f_heads"]
    hd = d // f_heads
    qk, vd = m_heads * dqk, m_heads * dv
    mtok = b * t
    xf = x.reshape(mtok, d)
    pf = p.reshape(depth, mtok, p.shape[-1])
    chunk = _tile(t, 256)
    m_states, f_rows = [], []

    def ffn(xf, g, name, i):
        key = (name, i)
        if key in w["ffn_bf16"]:
            return _ffn(xf, g, w["ffn_bf16"][key], i)
        out, w["ffn_bf16"][key] = _ffn(xf, g, (w[name + "_in"], w[name + "_out"]), i)
        return out

    for i in range(depth):
        gains = w["norm_gains"][i]
        gain = lambda r: gains[r].reshape(1, d)
        j = i // 2
        xf = ffn(xf, gain(0), "ffn1", i)
        if i % 2 == 0:
            z, g_rows, g_cols = _mlstm_in_proj(xf, gain(1), w["mlstm_w_in"], j, 2 * qk + 2 * vd,
                                               2 * m_heads)
            ng = g_rows.shape[0]
            g_rows = jnp.swapaxes(g_rows.reshape(ng, b, t), 0, 1)
            if mlstm_init is None:
                c0 = jnp.zeros((b, m_heads, dqk, dv), F32)
                n0 = jnp.zeros((b, m_heads, dqk), F32)
                m0 = jnp.zeros((b, m_heads), F32)
            else:
                c0, n0, m0 = mlstm_init[0][j], mlstm_init[1][j], mlstm_init[2][j]
            bias = w["mlstm_b_gates"][j].reshape(2 * m_heads)
            a, c_new, n_new, m_new = _mlstm(
                z.reshape(b, t, -1), g_cols.reshape(b, t, LANES), g_rows,
                jnp.pad(bias, (0, LANES - 2 * m_heads)).reshape(1, LANES),
                _pad_rows(bias.reshape(-1, 1), ng),
                w["mlstm_g_h"][j].reshape(1, vd),
                c0, n0.reshape(b, m_heads, 1, dqk),
                jnp.broadcast_to(m0[:, :, None, None], (b, m_heads, 1, LANES)), chunk)
            m_states.append((c_new, n_new.reshape(b, m_heads, dqk), m_new[:, :, 0, 0]))
            xf = _out_proj(a.reshape(mtok, vd), w["mlstm_w_out"], xf, j)
        else:
            qb, kf, kb, vf, vb, og, f_pre = _fox_in_proj(
                xf, gain(1), w["fox_w_in"], w["fox_g_qk"][j].reshape(2, 1, hd), j, hd,
                hd ** -0.5 * LOG2E)
            ng = f_pre.shape[0]
            f_pre = jnp.swapaxes(f_pre.reshape(ng, b, t), 0, 1)
            bias_col = _pad_rows(w["fox_b_f"][j].reshape(-1, 1), ng)
            shp = (b, t, d)
            if fox_past is None:
                lf_row, c_row, lf_col, c_col = _forget_cumsum(f_pre, bias_col, True, False, True)
                a = _fox_prompt_attn(qb.reshape(shp), kb.reshape(shp), vb.reshape(shp),
                                     og.reshape(shp), c_col, c_row, f_heads)
                lf = lf_col[:, :, :f_heads]
            else:
                lf_row, c_row = _forget_cumsum(f_pre, bias_col, True, False, False)
                lf_past_row = jnp.swapaxes(fox_past[2][j].astype(F32), 1, 2)
                lf_past_row = jnp.pad(lf_past_row, ((0, 0), (0, ng - f_heads), (0, 0)))
                _, _, c3_past = _forget_cumsum(lf_past_row, bias_col, False, True, True, f_heads)
                a = _fox_sample_attn(qb.reshape(shp), fox_past[0], fox_past[1], j,
                                     kb.reshape(shp), vb.reshape(shp), og.reshape(shp),
                                     _col_layout(c_row), c3_past, _c3_layout(c_row, f_heads))
                lf = jnp.swapaxes(lf_row[:, :f_heads, :], 1, 2)
            f_rows.append((kf.reshape(b, t, f_heads, hd), vf.reshape(b, t, f_heads, hd), lf))
            xf = _out_proj(a.reshape(mtok, d), w["fox_w_out"], xf, j)
        xf = ffn(xf, gain(2), "ffn2", i)
        xf = _ple(xf, gain(3), w["ple_gate"], pf, w["ple_proj"], i)
    stack = lambda k: jnp.stack([s[k] for s in m_states])
    stack_f = lambda k: jnp.stack([r[k] for r in f_rows])
    return (xf.reshape(b, t, d), stack(0), stack(1), stack(2), stack_f(0), stack_f(1), stack_f(2))


def kernel(x_prompt, x_sample, p_prompt, p_sample, state_mlstm_C, state_mlstm_n, state_mlstm_m, cache_fox_k, cache_fox_v, cache_fox_lf, norm_gains, ffn1_in, ffn1_out, ffn2_in, ffn2_out, ple_gate, ple_proj, mlstm_w_in, mlstm_b_gates, mlstm_g_h, mlstm_w_out, fox_w_in, fox_b_f, fox_g_qk, fox_w_out):
    m_heads, dqk, dv = state_mlstm_C.shape[2:]
    w = {
        "m_heads": m_heads, "dqk": dqk, "dv": dv, "f_heads": cache_fox_lf.shape[-1],
        "norm_gains": norm_gains, "mlstm_b_gates": mlstm_b_gates, "mlstm_g_h": mlstm_g_h,
        "fox_b_f": fox_b_f, "fox_g_qk": fox_g_qk,
        "ffn1_in": ffn1_in, "ffn1_out": ffn1_out, "ffn2_in": ffn2_in, "ffn2_out": ffn2_out,
        "ffn_bf16": {},
        "ple_gate": ple_gate.astype(BF16), "ple_proj": ple_proj.astype(BF16),
        "mlstm_w_in": mlstm_w_in.astype(BF16), "mlstm_w_out": mlstm_w_out.astype(BF16),
        "fox_w_in": fox_w_in.astype(BF16), "fox_w_out": fox_w_out.astype(BF16),
    }
    y_s, s_c, s_n, s_m, s_k, s_v, s_lf = _run_trunk(
        x_sample, p_sample, w, (state_mlstm_C, state_mlstm_n, state_mlstm_m),
        (cache_fox_k, cache_fox_v, cache_fox_lf))
    y_p, p_c, p_n, p_m, p_k, p_v, p_lf = _run_trunk(x_prompt, p_prompt, w, None, None)
    return (y_p, y_s, p_c, p_n, p_m, p_k, p_v, p_lf, s_c, s_n, s_m, s_k, s_v, s_lf)
```

```python
import functools

import jax
import jax.numpy as jnp
from jax import lax
from jax.experimental import pallas as pl
from jax.experimental.pallas import tpu as pltpu

F32 = jnp.float32
BF16 = jnp.bfloat16
NORM_EPS = 1e-6
FFN_HALF = 0.5
V7X_VMEM_BYTES = 64 * 1024 * 1024
VMEM_LIMIT = V7X_VMEM_BYTES - 4 * 1024 * 1024
LANES = 128
SUBLANES = 8
NEG_INF = float("-inf")
LOG2E = 1.4426950408889634
ATTN_KEY_BATCH = 4


def _params(*sem):
    return pltpu.CompilerParams(dimension_semantics=sem, vmem_limit_bytes=VMEM_LIMIT)


def _tile(n, pref):
    if n <= pref:
        return n
    for t in range(pref - pref % LANES, 0, -LANES):
        if n % t == 0:
            return t
    raise ValueError((n, pref))


def _round_up(n, k):
    return (n + k - 1) // k * k


def _rms(x, g):
    ms = jnp.mean(x * x, axis=-1, keepdims=True)
    return x * lax.rsqrt(ms + NORM_EPS) * g


def _sigmoid(x):
    return 1.0 / (1.0 + jnp.exp(-x))


def _log_sigmoid(x):
    return jnp.minimum(x, 0.0) - jnp.log(1.0 + jnp.exp(-jnp.abs(x)))


def _dot(a, b):
    return jnp.dot(a, b, preferred_element_type=F32)


def _dot_nt(a, b):
    return lax.dot_general(a, b, (((1,), (1,)), ((), ())), preferred_element_type=F32)


def _dot_tn(a, b):
    return lax.dot_general(a, b, (((0,), (0,)), ((), ())), preferred_element_type=F32)


def _dot_f32(a, b):
    return jnp.dot(a, b, preferred_element_type=F32, precision=lax.Precision.HIGHEST)


def _ffn_body(x_ref, g_ref, wg_ref, wu_ref, wo_ref, out_ref, *rest, col_chunk):
    xn_ref = rest[-1]
    f = pl.program_id(1)

    @pl.when(f == 0)
    def _():
        x = x_ref[...]
        xn_ref[...] = _rms(x, g_ref[...]).astype(BF16)
        out_ref[...] = x

    if len(rest) > 1:
        wg, wu, wo = (r[...].astype(BF16) for r in (wg_ref, wu_ref, wo_ref))
        for w, cast_ref in zip((wg, wu, wo), rest[:-1]):
            cast_ref[...] = w
        wo_cols = lambda sl: wo[:, sl]
    else:
        wg, wu = wg_ref[...], wu_ref[...]
        wo_cols = lambda sl: wo_ref[:, sl]
    xn = xn_ref[...]
    g = _dot(xn, wg)
    u = _dot(xn, wu)
    h = (g * _sigmoid(g) * u * FFN_HALF).astype(BF16)
    d = out_ref.shape[1]
    for c in range(d // col_chunk):
        sl = slice(c * col_chunk, (c + 1) * col_chunk)
        out_ref[:, sl] += _dot(h, wo_cols(sl))


def _ffn(x, gain, weights, layer):
    m, d = x.shape
    cast = len(weights) == 2
    f_dim = weights[1].shape[1] if cast else weights[2].shape[0]
    tm = _tile(m, 1024)
    tf = _tile(f_dim, 512)
    nf = f_dim // tf
    x_specs = [
        pl.BlockSpec((tm, d), lambda i, f: (i, 0)),
        pl.BlockSpec((1, d), lambda i, f: (0, 0)),
    ]
    col_blk = pl.BlockSpec((d, tf), lambda i, f: (0, f))
    row_blk = pl.BlockSpec((tf, d), lambda i, f: (f, 0))
    out_shape = jax.ShapeDtypeStruct((m, d), F32)
    out_spec = pl.BlockSpec((tm, d), lambda i, f: (i, 0))
    if cast:
        assert m == tm, "the cast outputs are written once per hidden tile"
        w_in, w_out = weights
        w_specs = [
            pl.BlockSpec((None, d, tf), lambda i, f: (layer, 0, f)),
            pl.BlockSpec((None, d, tf), lambda i, f: (layer, 0, f + nf)),
            pl.BlockSpec((None, tf, d), lambda i, f: (layer, f, 0)),
        ]
        args = (w_in, w_in, w_out)
        out_shape = (out_shape, jax.ShapeDtypeStruct((d, f_dim), BF16),
                     jax.ShapeDtypeStruct((d, f_dim), BF16), jax.ShapeDtypeStruct((f_dim, d), BF16))
        out_spec = (out_spec, col_blk, col_blk, row_blk)
    else:
        w_specs = [col_blk, col_blk, row_blk]
        args = tuple(weights)
    res = pl.pallas_call(
        functools.partial(_ffn_body, col_chunk=_tile(d, 512)),
        out_shape=out_shape,
        grid=(m // tm, nf),
        in_specs=x_specs + w_specs,
        out_specs=out_spec,
        scratch_shapes=[pltpu.VMEM((tm, d), BF16)],
        compiler_params=_params("parallel", "arbitrary"),
        name="ffn_cast" if cast else "ffn",
    )(x, gain, *args)
    return (res[0], tuple(res[1:])) if cast else res


def _rows_to_cols(rows):
    ng, n = rows.shape
    padded = jnp.concatenate([rows, jnp.zeros((LANES - ng, n), F32)], axis=0)
    return jnp.transpose(padded)


def _norm_and_gates(x_ref, g_ref, wgate_ref, xn_ref, rows_ref, cols_ref, n_gates):
    xn = _rms(x_ref[...], g_ref[...]).astype(BF16)
    xn_ref[...] = xn
    cols = _dot(xn, wgate_ref[...])
    lane = lax.broadcasted_iota(jnp.int32, cols.shape, 1)
    cols = jnp.where(lane < n_gates, cols, 0.0)
    rows_ref[...] = jnp.transpose(cols)[:rows_ref.shape[0], :]
    if cols_ref is not None:
        cols_ref[...] = cols


def _mlstm_in_body(x_ref, g_ref, w_ref, wgate_ref, z_ref, rows_ref, cols_ref, xn_ref, *, n_gates):
    @pl.when(pl.program_id(1) == 0)
    def _():
        _norm_and_gates(x_ref, g_ref, wgate_ref, xn_ref, rows_ref, cols_ref, n_gates)

    z_ref[...] = _dot(xn_ref[...], w_ref[...])


def _mlstm_in_proj(x, gain, w, layer, ncols, n_gates):
    m, d = x.shape
    ng = _round_up(n_gates, SUBLANES)
    tm = _tile(m, 1024)
    tn = _tile(ncols, 1024)
    assert ncols % LANES == 0 and n_gates <= LANES
    return pl.pallas_call(
        functools.partial(_mlstm_in_body, n_gates=n_gates),
        out_shape=(jax.ShapeDtypeStruct((m, ncols), F32), jax.ShapeDtypeStruct((ng, m), F32),
                   jax.ShapeDtypeStruct((m, LANES), F32)),
        grid=(m // tm, ncols // tn),
        in_specs=[
            pl.BlockSpec((tm, d), lambda i, j: (i, 0)),
            pl.BlockSpec((1, d), lambda i, j: (0, 0)),
            pl.BlockSpec((None, d, tn), lambda i, j: (layer, 0, j)),
            pl.BlockSpec((None, d, LANES), lambda i, j: (layer, 0, ncols // LANES)),
        ],
        out_specs=(pl.BlockSpec((tm, tn), lambda i, j: (i, j)),
                   pl.BlockSpec((ng, tm), lambda i, j: (0, i)),
                   pl.BlockSpec((tm, LANES), lambda i, j: (i, 0))),
        scratch_shapes=[pltpu.VMEM((tm, d), BF16)],
        compiler_params=_params("parallel", "arbitrary"),
        name="mlstm_in_proj",
    )(x, gain, w, w)


def _fox_in_body(*refs, seg, hd, q_scale, n_gates, stacked):
    x_ref, g_ref, w_ref, wgate_ref, gq_ref, gk_ref = refs[:6]
    kprev_ref, vprev_ref = refs[6:8] if stacked else (None, None)
    q_ref, kf_ref, kb_ref, vf_ref, vb_ref, og_ref, rows_ref, xn_ref = refs[-8:]
    j = pl.program_id(1)

    @pl.when(j == 0)
    def _():
        _norm_and_gates(x_ref, g_ref, wgate_ref, xn_ref, rows_ref, None, n_gates)

    z = _dot(xn_ref[...], w_ref[...])
    heads = [slice(c * hd, (c + 1) * hd) for c in range(z.shape[1] // hd)]

    @pl.when(j < seg)
    def _():
        for sl in heads:
            q_ref[:, sl] = (_rms(z[:, sl], gq_ref[...]) * q_scale).astype(BF16)

    @pl.when(jnp.logical_and(j >= seg, j < 2 * seg))
    def _():
        if stacked:
            kf_ref[:stacked] = kprev_ref[...]
        for sl in heads:
            y = _rms(z[:, sl], gk_ref[...])
            kf_ref[stacked, :, sl] = y
            kb_ref[:, sl] = y.astype(BF16)

    @pl.when(jnp.logical_and(j >= 2 * seg, j < 3 * seg))
    def _():
        if stacked:
            vf_ref[:stacked] = vprev_ref[...]
        vf_ref[stacked] = z
        vb_ref[...] = z.astype(BF16)

    @pl.when(j >= 3 * seg)
    def _():
        og_ref[...] = z


def _fox_in_proj(x, gain, w, g_qk, layer, hd, q_scale, prev_kv=None):
    m, d = x.shape
    n_gates = d // hd
    ng = _round_up(n_gates, SUBLANES)
    stacked = 0 if prev_kv is None else prev_kv[0].shape[0]
    tm = _tile(m, 1024)
    tn = _tile(d, 512)
    seg = d // tn
    col = lambda s: (lambda i, j: (i, jnp.clip(j - s * seg, 0, seg - 1)))
    seg_spec = lambda s: pl.BlockSpec((tm, tn), col(s))
    stack_spec = lambda n, s: pl.BlockSpec((n, tm, tn), lambda i, j: (0,) + col(s)(i, j))
    f32_out = jax.ShapeDtypeStruct((m, d), F32)
    bf16_out = jax.ShapeDtypeStruct((m, d), BF16)
    stack_out = jax.ShapeDtypeStruct((stacked + 1, m, d), F32)
    in_specs = [
        pl.BlockSpec((tm, d), lambda i, j: (i, 0), pipeline_mode=pl.Buffered(1)),
        pl.BlockSpec((1, d), lambda i, j: (0, 0)),
        pl.BlockSpec((None, d, tn), lambda i, j: (layer, 0, j)),
        pl.BlockSpec((None, d, LANES), lambda i, j: (layer, 0, 4 * d // LANES)),
        pl.BlockSpec((None, 1, hd), lambda i, j: (0, 0, 0)),
        pl.BlockSpec((None, 1, hd), lambda i, j: (1, 0, 0)),
    ]
    args = [x, gain, w, w, g_qk, g_qk]
    if stacked:
        in_specs += [stack_spec(stacked, 1), stack_spec(stacked, 2)]
        args += list(prev_kv)
    return pl.pallas_call(
        functools.partial(_fox_in_body, seg=seg, hd=hd, q_scale=q_scale, n_gates=n_gates,
                          stacked=stacked),
        out_shape=(bf16_out, stack_out, bf16_out, stack_out, bf16_out, f32_out,
                   jax.ShapeDtypeStruct((ng, m), F32)),
        grid=(m // tm, 4 * seg),
        in_specs=in_specs,
        out_specs=(seg_spec(0), stack_spec(stacked + 1, 1), seg_spec(1), stack_spec(stacked + 1, 2),
                   seg_spec(2), seg_spec(3), pl.BlockSpec((ng, tm), lambda i, j: (0, i))),
        scratch_shapes=[pltpu.VMEM((tm, d), BF16)],
        compiler_params=_params("parallel", "arbitrary"),
        name="fox_in_proj",
    )(*args)


def _out_body(a_ref, w_ref, x_ref, o_ref):
    o_ref[...] = x_ref[...] + _dot(a_ref[...], w_ref[...])


def _out_proj(a, w, x, layer):
    m, k = a.shape
    d = x.shape[1]
    tm = _tile(m, 1024)
    tn = _tile(d, 1024)
    return pl.pallas_call(
        _out_body,
        out_shape=jax.ShapeDtypeStruct((m, d), F32),
        grid=(m // tm, d // tn),
        in_specs=[
            pl.BlockSpec((tm, k), lambda i, j: (i, 0)),
            pl.BlockSpec((None, k, tn), lambda i, j: (layer, 0, j)),
            pl.BlockSpec((tm, tn), lambda i, j: (i, j)),
        ],
        out_specs=pl.BlockSpec((tm, tn), lambda i, j: (i, j)),
        compiler_params=_params("parallel", "arbitrary"),
        name="out_proj",
    )(a, w, x)


def _ple_body(x_ref, g_ref, wg_ref, p_ref, wp_ref, o_ref, *, col_chunk):
    x = x_ref[...]
    xn = _rms(x, g_ref[...]).astype(BF16)
    pb = p_ref[...].astype(BF16)
    for c in range(x.shape[1] // col_chunk):
        sl = slice(c * col_chunk, (c + 1) * col_chunk)
        gate = _sigmoid(_dot(xn, wg_ref[:, sl]))
        o_ref[:, sl] = x[:, sl] + gate * _dot(pb, wp_ref[:, sl])


def _ple(x, gain, w_gate, p, w_proj, layer):
    m, d = x.shape
    pd = p.shape[-1]
    tm = _tile(m, 512)
    return pl.pallas_call(
        functools.partial(_ple_body, col_chunk=_tile(d, 512)),
        out_shape=jax.ShapeDtypeStruct((m, d), F32),
        grid=(m // tm,),
        in_specs=[
            pl.BlockSpec((tm, d), lambda i: (i, 0)),
            pl.BlockSpec((1, d), lambda i: (0, 0)),
            pl.BlockSpec((None, d, d), lambda i: (layer, 0, 0), pipeline_mode=pl.Buffered(1)),
            pl.BlockSpec((None, tm, pd), lambda i: (layer, i, 0)),
            pl.BlockSpec((None, pd, d), lambda i: (layer, 0, 0), pipeline_mode=pl.Buffered(1)),
        ],
        out_specs=pl.BlockSpec((tm, d), lambda i: (i, 0)),
        compiler_params=_params("parallel"),
        name="ple",
    )(x, gain, w_gate, p, w_proj)


def _mlstm_body(q_ref, k_ref, v_ref, o_ref, gc_ref, gr_ref, bc_ref, br_ref, gh_ref,
                c0_ref, n0_ref, m0_ref, a_ref, c_ref, n_ref, m_ref, *, heads, dqk, dv):
    ci = pl.program_id(1)
    L = q_ref.shape[0]

    @pl.when(ci == 0)
    def _():
        c_ref[...] = c0_ref[...]
        n_ref[...] = n0_ref[...]
        m_ref[...] = m0_ref[...]

    row = lax.broadcasted_iota(jnp.int32, (L, L), 0)
    col = lax.broadcasted_iota(jnp.int32, (L, L), 1)
    causal = row >= col
    tril = causal.astype(F32)
    triu = (row <= col).astype(F32)

    gc = gc_ref[...] + bc_ref[...]
    gr = gr_ref[...] + br_ref[...]
    b_col = _dot_f32(tril, _log_sigmoid(gc))
    b_row = _dot_f32(_log_sigmoid(gr), triu)
    scale = dqk ** -0.5

    for h in range(heads):
        q = q_ref[:, h * dqk:(h + 1) * dqk]
        k = k_ref[:, h * dqk:(h + 1) * dqk] * scale
        v = v_ref[:, h * dv:(h + 1) * dv].astype(BF16)
        qb = q.astype(BF16)
        cst = c_ref[h]
        nst = n_ref[h]
        m_prev = m_ref[h][:, :1]

        bc = b_col[:, heads + h:heads + h + 1]
        ic = gc[:, h:h + 1]
        brow = b_row[heads + h:heads + h + 1, :]
        irow = gr[h:h + 1, :]

        dmat = jnp.where(causal, bc - brow + irow, NEG_INF)
        g = bc + m_prev
        m_t = jnp.maximum(g, jnp.max(dmat, axis=-1, keepdims=True))
        s = _dot_nt(qb, k.astype(BF16)) * jnp.exp(dmat - m_t)
        inter = jnp.exp(g - m_t)
        num = _dot(s.astype(BF16), v) + inter * _dot(qb, cst.astype(BF16))
        den = jnp.sum(s, axis=-1, keepdims=True) + inter * jnp.sum(q * nst, axis=-1, keepdims=True)
        hval = num / jnp.maximum(jnp.abs(den), jnp.exp(-m_t))

        hn = _rms(hval, gh_ref[:, h * dv:(h + 1) * dv])
        og = o_ref[:, h * dv:(h + 1) * dv]
        a_ref[:, h * dv:(h + 1) * dv] = (hn * _sigmoid(og)).astype(BF16)

        b_end = bc[L - 1:L, :]
        m_new = m_t[L - 1:L, :]
        decay = jnp.exp(b_end + m_prev - m_new)
        wa = jnp.exp(b_end - bc + ic - m_new)
        kw = k * wa
        c_ref[h] = decay * cst + _dot_tn(kw.astype(BF16), v)
        n_ref[h] = decay * nst + jnp.sum(kw, axis=0, keepdims=True)
        m_ref[h] = jnp.broadcast_to(m_new, (1, LANES))


def _mlstm(z, gates_col, gates_row, bias_lanes, bias_rows, g_h, c0, n0, m0, chunk):
    b, t, _ = z.shape
    heads, dqk, dv = c0.shape[1:]
    qk = heads * dqk
    vd = heads * dv
    assert (2 * qk) % vd == 0
    v_blk = 2 * qk // vd
    ng = gates_row.shape[1]
    nc = t // chunk
    st4 = lambda i, c: (i, 0, 0, 0)
    return pl.pallas_call(
        functools.partial(_mlstm_body, heads=heads, dqk=dqk, dv=dv),
        out_shape=(
            jax.ShapeDtypeStruct((b, t, vd), BF16),
            jax.ShapeDtypeStruct((b, heads, dqk, dv), F32),
            jax.ShapeDtypeStruct((b, heads, 1, dqk), F32),
            jax.ShapeDtypeStruct((b, heads, 1, LANES), F32),
        ),
        grid=(b, nc),
        in_specs=[
            pl.BlockSpec((None, chunk, qk), lambda i, c: (i, c, 0)),
            pl.BlockSpec((None, chunk, qk), lambda i, c: (i, c, 1)),
            pl.BlockSpec((None, chunk, vd), lambda i, c: (i, c, v_blk)),
            pl.BlockSpec((None, chunk, vd), lambda i, c: (i, c, v_blk + 1)),
            pl.BlockSpec((None, chunk, LANES), lambda i, c: (i, c, 0)),
            pl.BlockSpec((None, ng, chunk), lambda i, c: (i, 0, c)),
            pl.BlockSpec((1, LANES), lambda i, c: (0, 0)),
            pl.BlockSpec((ng, 1), lambda i, c: (0, 0)),
            pl.BlockSpec((1, vd), lambda i, c: (0, 0)),
            pl.BlockSpec((None, heads, dqk, dv), st4),
            pl.BlockSpec((None, heads, 1, dqk), st4),
            pl.BlockSpec((None, heads, 1, LANES), st4),
        ],
        out_specs=(
            pl.BlockSpec((None, chunk, vd), lambda i, c: (i, c, 0)),
            pl.BlockSpec((None, heads, dqk, dv), st4),
            pl.BlockSpec((None, heads, 1, dqk), st4),
            pl.BlockSpec((None, heads, 1, LANES), st4),
        ),
        compiler_params=_params("parallel", "arbitrary"),
        name="mlstm",
    )(z, z, z, z, gates_col, gates_row, bias_lanes, bias_rows, g_h, c0, n0, m0)


def _cumsum_body(f_ref, b_ref, lf_ref, c_ref, *col_refs, blk, activate, anchor_end, heads):
    t = f_ref.shape[1]
    row = lax.broadcasted_iota(jnp.int32, (blk, blk), 0)
    col = lax.broadcasted_iota(jnp.int32, (blk, blk), 1)
    triu = (row <= col).astype(F32)
    carry = jnp.zeros((f_ref.shape[0], 1), F32)
    for i in range(t // blk):
        sl = slice(i * blk, (i + 1) * blk)
        seg = f_ref[:, sl]
        if activate:
            seg = _log_sigmoid(seg + b_ref[...])
        lf_ref[:, sl] = seg
        cs = _dot_f32(seg, triu) + carry
        c_ref[:, sl] = cs
        carry = cs[:, blk - 1:blk]
        if col_refs and not anchor_end:
            col_refs[0][sl, :] = _rows_to_cols(seg)
            col_refs[1][sl, :] = _rows_to_cols(cs)
    if anchor_end:
        c_ref[...] = c_ref[...] - carry
        if col_refs:
            sub = lax.broadcasted_iota(jnp.int32, (SUBLANES, blk), 0)
            ones3 = jnp.where(sub < 3, 1.0, 0.0)
            for i in range(t // blk):
                sl = slice(i * blk, (i + 1) * blk)
                c = c_ref[:heads, sl]
                col_refs[0][sl, :] = _rows_to_cols(jnp.concatenate([c, c, c, ones3], axis=0))


def _forget_cumsum(f_rows, bias_col, activate, anchor_end, emit_cols, heads=None):
    b, ng, t = f_rows.shape
    blk = _tile(t, 512)
    row_spec = pl.BlockSpec((None, ng, t), lambda i: (i, 0, 0))
    row_shape = jax.ShapeDtypeStruct((b, ng, t), F32)
    out_shape, out_specs = (row_shape, row_shape), (row_spec, row_spec)
    if emit_cols:
        col_shape = jax.ShapeDtypeStruct((b, t, LANES), F32)
        col_spec = pl.BlockSpec((None, t, LANES), lambda i: (i, 0, 0))
        n_cols = 1 if anchor_end else 2
        out_shape, out_specs = out_shape + (col_shape,) * n_cols, out_specs + (col_spec,) * n_cols
    return pl.pallas_call(
        functools.partial(_cumsum_body, blk=blk, activate=activate, anchor_end=anchor_end, heads=heads),
        out_shape=out_shape,
        grid=(b,),
        in_specs=[row_spec, pl.BlockSpec((ng, 1), lambda i: (0, 0))],
        out_specs=out_specs,
        compiler_params=_params("parallel"),
        name="forget_cumsum",
    )(f_rows, bias_col)


def _split3(c):
    hi = c.astype(BF16).astype(F32)
    r = c - hi
    mid = r.astype(BF16).astype(F32)
    return hi, mid, r - mid


def _bias_lanes(c, query_side):
    hi, mid, lo = _split3(c if query_side else -c)
    lane = lax.broadcasted_iota(jnp.int32, (c.shape[0], LANES), 1)
    o = 0 if query_side else 3
    terms = jnp.where(lane == o, hi, jnp.where(lane == o + 1, mid, jnp.where(lane == o + 2, lo, 0.0)))
    ones = jnp.logical_and(lane >= 3 - o, lane < 6 - o)
    return jnp.where(ones, 1.0, terms).astype(BF16)


def _online_softmax_step(s, v, m, l, acc):
    m_new = jnp.maximum(m, jnp.max(s, axis=-1, keepdims=True))
    p = jnp.exp2(s - m_new)
    alpha = jnp.exp2(m - m_new)
    l = alpha * l + jnp.sum(p, axis=-1, keepdims=True)
    acc = alpha * acc + _dot(p.astype(BF16), v)
    return m_new, l, acc


def _bias_rows(c, rows):
    hi, mid, lo = _split3(c)
    sub = lax.broadcasted_iota(jnp.int32, (rows, c.shape[1]), 0)
    terms = jnp.where(sub == 0, hi, jnp.where(sub == 1, mid, jnp.where(sub == 2, lo, 0.0)))
    return jnp.where(jnp.logical_and(sub >= 3, sub < 6), 1.0, terms).astype(BF16)


def _fox_prompt_body(q_ref, k_ref, v_ref, og_ref, cc_ref, cr_ref, a_ref, qt_ref, ka_ref, vt_ref, *, tq,
                     kbatch):
    h = pl.program_id(1)
    t, hd = q_ref.shape
    nt = t // tq
    for r in range(nt):
        rows = slice(r * tq, (r + 1) * tq)
        cc = cc_ref[rows, :]
        lane = lax.broadcasted_iota(jnp.int32, cc.shape, 1)
        c_col = jnp.sum(jnp.where(lane == h, cc, 0.0), axis=-1, keepdims=True) * LOG2E
        ka_ref[rows, :hd] = k_ref[rows, :]
        ka_ref[rows, hd:] = _bias_lanes(c_col, False)
        qt_ref[:hd, rows] = jnp.transpose(q_ref[rows, :].astype(F32)).astype(BF16)
        qt_ref[hd:, rows] = _bias_rows(cr_ref[r:r + 1, :] * LOG2E, LANES)
        vt_ref[:, rows] = jnp.transpose(v_ref[rows, :].astype(F32)).astype(BF16)

    key = lax.broadcasted_iota(jnp.int32, (tq, tq), 0)
    qry = lax.broadcasted_iota(jnp.int32, (tq, tq), 1)
    causal = key <= qry

    def tile_off(kj):
        return kj * tq if isinstance(kj, int) else pl.multiple_of(kj * tq, tq)

    for qi in range(nt):
        cols = slice(qi * tq, (qi + 1) * tq)
        qt = qt_ref[:, cols]

        def steps(carry, tiles):
            m, l, acc = carry
            scores = [_dot(ka_ref[pl.ds(tile_off(kj), tq), :], qt) for kj, _ in tiles]
            for s, (kj, diagonal) in zip(scores, tiles):
                if diagonal:
                    s = jnp.where(causal, s, NEG_INF)
                m_new = jnp.maximum(m, jnp.max(s, axis=0, keepdims=True))
                p = jnp.exp2(s - m_new)
                alpha = jnp.exp2(m - m_new)
                l = alpha * l + jnp.sum(p, axis=0, keepdims=True)
                acc = alpha * acc + _dot(vt_ref[:, pl.ds(tile_off(kj), tq)], p.astype(BF16))
                m = m_new
            return m, l, acc

        carry = (jnp.full((1, tq), NEG_INF, F32), jnp.zeros((1, tq), F32), jnp.zeros((hd, tq), F32))
        nb = qi // kbatch
        carry = lax.fori_loop(
            0, nb, lambda i, c: steps(c, [(i * kbatch + u, False) for u in range(kbatch)]), carry)
        tail = [(kj, False) for kj in range(nb * kbatch, qi)] + [(qi, True)]
        _, l, acc = steps(carry, tail)
        o = jnp.transpose(acc / l)
        a_ref[cols, :] = (o * _sigmoid(og_ref[cols, :])).astype(BF16)


def _fox_prompt_attn(qb, kb, vb, og, c_col, c_row, heads):
    b, t, d = qb.shape
    hd = d // heads
    tq = _tile(t, 512)
    nt = t // tq
    c_row4 = c_row.reshape(b, c_row.shape[1], nt, tq)
    head_blk = pl.BlockSpec((None, t, hd), lambda i, h: (i, 0, h))
    return pl.pallas_call(
        functools.partial(_fox_prompt_body, tq=tq, kbatch=ATTN_KEY_BATCH),
        out_shape=jax.ShapeDtypeStruct((b, t, d), BF16),
        grid=(b, heads),
        in_specs=[
            head_blk, head_blk, head_blk, head_blk,
            pl.BlockSpec((None, t, LANES), lambda i, h: (i, 0, 0)),
            pl.BlockSpec((None, None, nt, tq), lambda i, h: (i, h, 0, 0)),
        ],
        out_specs=head_blk,
        scratch_shapes=[pltpu.VMEM((hd + LANES, t), BF16), pltpu.VMEM((t, hd + LANES), BF16),
                        pltpu.VMEM((hd, t), BF16)],
        compiler_params=_params("parallel", "arbitrary"),
        name="fox_prompt_attn",
    )(qb, kb, vb, og, c_col, c_row4)


def _key_bias_lanes(c3, heads):
    lane = lax.broadcasted_iota(jnp.int32, c3.shape, 1)
    hi, mid, lo = _split3(c3 * jnp.where(lane < 3 * heads, LOG2E, 1.0))
    pick = jnp.where(lane < heads, hi, jnp.where(lane < 2 * heads, mid, jnp.where(lane < 3 * heads, lo, hi)))
    return pick.astype(BF16)


def _fox_sample_body(q_ref, kp_ref, vp_ref, kn_ref, vn_ref, og_ref, cq_ref, c3p_ref, c3n_ref,
                     a_ref, wq_ref, m_ref, l_ref, acc_ref, *, heads, hd, hg):
    j = pl.program_id(1)
    nj = pl.num_programs(1)
    tq = q_ref.shape[0]
    groups = heads // hg
    tk = kp_ref.shape[0] // heads

    @pl.when(j == 0)
    def _():
        m_ref[...] = jnp.full(m_ref.shape, NEG_INF, F32)
        l_ref[...] = jnp.zeros(l_ref.shape, F32)
        acc_ref[...] = jnp.zeros(acc_ref.shape, F32)
        wq_ref[...] = jnp.zeros(wq_ref.shape, BF16)
        lane = lax.broadcasted_iota(jnp.int32, (tq, LANES), 1)
        for g in range(groups):
            for i in range(hg):
                h = g * hg + i
                rows = slice(i * tq, (i + 1) * tq)
                wq_ref[g, rows, i * hd:(i + 1) * hd] = q_ref[:, h * hd:(h + 1) * hd]
                hi, mid, lo = _split3(cq_ref[:, h:h + 1] * LOG2E)
                own = jnp.logical_or(lane == h, jnp.logical_or(lane == heads + h, lane == 2 * heads + h))
                cq3 = jnp.where(lane == 3 * heads, hi,
                                jnp.where(lane == 3 * heads + 1, mid,
                                          jnp.where(lane == 3 * heads + 2, lo, 0.0)))
                wq_ref[g, rows, hg * hd:] = jnp.where(own, -1.0, cq3).astype(BF16)

    def update(g, s, v):
        m, l, acc = _online_softmax_step(s, v, m_ref[g], l_ref[g], acc_ref[g])
        m_ref[g] = m
        l_ref[g] = l
        acc_ref[g] = acc

    kbias = _key_bias_lanes(c3p_ref[...], heads)
    for g in range(groups):
        head_rows = lambda i: pl.ds(g * hg + i, tk, stride=heads)
        ks = [kp_ref[head_rows(i), :].astype(BF16) for i in range(hg)]
        vs = [vp_ref[head_rows(i), :].astype(BF16) for i in range(hg)]
        s = _dot_nt(wq_ref[g], jnp.concatenate(ks + [kbias], axis=1))
        update(g, s, jnp.concatenate(vs, axis=1))

    @pl.when(j == nj - 1)
    def _():
        row = lax.broadcasted_iota(jnp.int32, (hg * tq, tq), 0)
        col = lax.broadcasted_iota(jnp.int32, (hg * tq, tq), 1)
        causal = row % tq >= col
        nbias = _key_bias_lanes(c3n_ref[...], heads)
        for g in range(groups):
            cols = slice(g * hg * hd, (g + 1) * hg * hd)
            s = _dot_nt(wq_ref[g], jnp.concatenate([kn_ref[:, cols], nbias], axis=1))
            update(g, jnp.where(causal, s, NEG_INF), vn_ref[:, cols])
            for i in range(hg):
                rows = slice(i * tq, (i + 1) * tq)
                sl = slice((g * hg + i) * hd, (g * hg + i + 1) * hd)
                o = acc_ref[g, rows, i * hd:(i + 1) * hd] / l_ref[g, rows, :]
                a_ref[:, sl] = (o * _sigmoid(og_ref[:, sl])).astype(BF16)


def _fox_sample_attn(qb, k_past, v_past, layer, kb, vb, og, cq_col, c3_past, c3_new):
    b, tq, d = qb.shape
    p, heads, hd = k_past.shape[2:]
    assert 3 * heads + 3 <= LANES
    hg = max(1, min(heads, 512 // hd))
    assert heads % hg == 0
    tk = _tile(p, 1024)
    new_blk = pl.BlockSpec((None, tq, d), lambda i, j: (i, 0, 0))
    past_blk = pl.BlockSpec((None, None, tk * heads, hd), lambda i, j: (layer, i, j, 0))
    flat = lambda a: a.reshape(a.shape[:2] + (p * heads, hd))
    return pl.pallas_call(
        functools.partial(_fox_sample_body, heads=heads, hd=hd, hg=hg),
        out_shape=jax.ShapeDtypeStruct((b, tq, d), BF16),
        grid=(b, p // tk),
        in_specs=[
            new_blk, past_blk, past_blk, new_blk, new_blk, new_blk,
            pl.BlockSpec((None, tq, LANES), lambda i, j: (i, 0, 0)),
            pl.BlockSpec((None, tk, LANES), lambda i, j: (i, j, 0)),
            pl.BlockSpec((None, tq, LANES), lambda i, j: (i, 0, 0)),
        ],
        out_specs=new_blk,
        scratch_shapes=[
            pltpu.VMEM((heads // hg, hg * tq, hg * hd + LANES), BF16),
            pltpu.VMEM((heads // hg, hg * tq, 1), F32),
            pltpu.VMEM((heads // hg, hg * tq, 1), F32),
            pltpu.VMEM((heads // hg, hg * tq, hg * hd), F32),
        ],
        compiler_params=_params("parallel", "arbitrary"),
        name="fox_sample_attn",
    )(qb, flat(k_past), flat(v_past), kb, vb, og, cq_col, c3_past, c3_new)


def _pad_rows(a, rows):
    return jnp.pad(a, ((0, rows - a.shape[0]),) + ((0, 0),) * (a.ndim - 1))


def _col_layout(rows):
    c = jnp.swapaxes(rows, 1, 2)
    return jnp.pad(c, ((0, 0), (0, 0), (0, LANES - c.shape[-1])))


def _c3_layout(rows, heads):
    c = jnp.swapaxes(rows[:, :heads, :], 1, 2)
    c3 = jnp.concatenate([c, c, c, jnp.ones(c.shape[:2] + (3,), F32)], axis=-1)
    return jnp.pad(c3, ((0, 0), (0, 0), (0, LANES - c3.shape[-1])))


def _run_trunk(x, p, w, mlstm_init, fox_past):
    b, t, d = x.shape
    depth = w["norm_gains"].shape[0]
    m_heads, dqk, dv = w["m_heads"], w["dqk"], w["dv"]
    f_heads = w["f_heads"]
    hd = d // f_heads
    qk, vd = m_heads * dqk, m_heads * dv
    mtok = b * t
    xf = x.reshape(mtok, d)
    pf = p.reshape(depth, mtok, p.shape[-1])
    chunk = _tile(t, 256)
    m_states, f_rows, kv_stacks = [], [], None

    def ffn(xf, g, name, i):
        key = (name, i)
        if key in w["ffn_bf16"]:
            return _ffn(xf, g, w["ffn_bf16"][key], i)
        out, w["ffn_bf16"][key] = _ffn(xf, g, (w[name + "_in"], w[name + "_out"]), i)
        return out

    for i in range(depth):
        gains = w["norm_gains"][i]
        gain = lambda r: gains[r].reshape(1, d)
        j = i // 2
        xf = ffn(xf, gain(0), "ffn1", i)
        if i % 2 == 0:
            z, g_rows, g_cols = _mlstm_in_proj(xf, gain(1), w["mlstm_w_in"], j, 2 * qk + 2 * vd,
                                               2 * m_heads)
            ng = g_rows.shape[0]
            g_rows = jnp.swapaxes(g_rows.reshape(ng, b, t), 0, 1)
            if mlstm_init is None:
                c0 = jnp.zeros((b, m_heads, dqk, dv), F32)
                n0 = jnp.zeros((b, m_heads, dqk), F32)
                m0 = jnp.zeros((b, m_heads), F32)
            else:
                c0, n0, m0 = mlstm_init[0][j], mlstm_init[1][j], mlstm_init[2][j]
            bias = w["mlstm_b_gates"][j].reshape(2 * m_heads)
            a, c_new, n_new, m_new = _mlstm(
                z.reshape(b, t, -1), g_cols.reshape(b, t, LANES), g_rows,
                jnp.pad(bias, (0, LANES - 2 * m_heads)).reshape(1, LANES),
                _pad_rows(bias.reshape(-1, 1), ng),
                w["mlstm_g_h"][j].reshape(1, vd),
                c0, n0.reshape(b, m_heads, 1, dqk),
                jnp.broadcast_to(m0[:, :, None, None], (b, m_heads, 1, LANES)), chunk)
            m_states.append((c_new, n_new.reshape(b, m_heads, dqk), m_new[:, :, 0, 0]))
            xf = _out_proj(a.reshape(mtok, vd), w["mlstm_w_out"], xf, j)
        else:
            qb, k_stack, kb, v_stack, vb, og, f_pre = _fox_in_proj(
                xf, gain(1), w["fox_w_in"], w["fox_g_qk"][j].reshape(2, 1, hd), j, hd,
                hd ** -0.5 * LOG2E, kv_stacks)
            kv_stacks = (k_stack, v_stack)
            ng = f_pre.shape[0]
            f_pre = jnp.swapaxes(f_pre.reshape(ng, b, t), 0, 1)
            bias_col = _pad_rows(w["fox_b_f"][j].reshape(-1, 1), ng)
            shp = (b, t, d)
            if fox_past is None:
                lf_row, c_row, lf_col, c_col = _forget_cumsum(f_pre, bias_col, True, False, True)
                a = _fox_prompt_attn(qb.reshape(shp), kb.reshape(shp), vb.reshape(shp),
                                     og.reshape(shp), c_col, c_row, f_heads)
                lf = lf_col[:, :, :f_heads]
            else:
                lf_row, c_row = _forget_cumsum(f_pre, bias_col, True, False, False)
                lf_past_row = jnp.swapaxes(fox_past[2][j].astype(F32), 1, 2)
                lf_past_row = jnp.pad(lf_past_row, ((0, 0), (0, ng - f_heads), (0, 0)))
                _, _, c3_past = _forget_cumsum(lf_past_row, bias_col, False, True, True, f_heads)
                a = _fox_sample_attn(qb.reshape(shp), fox_past[0], fox_past[1], j,
                                     kb.reshape(shp), vb.reshape(shp), og.reshape(shp),
                                     _col_layout(c_row), c3_past, _c3_layout(c_row, f_heads))
                lf = jnp.swapaxes(lf_row[:, :f_heads, :], 1, 2)
            f_rows.append(lf)
            xf = _out_proj(a.reshape(mtok, d), w["fox_w_out"], xf, j)
        xf = ffn(xf, gain(2), "ffn2", i)
        xf = _ple(xf, gain(3), w["ple_gate"], pf, w["ple_proj"], i)
    stack = lambda k: jnp.stack([s[k] for s in m_states])
    k_all, v_all = (a.reshape(a.shape[0], b, t, f_heads, hd) for a in kv_stacks)
    return (xf.reshape(b, t, d), stack(0), stack(1), stack(2), k_all, v_all, jnp.stack(f_rows))


def kernel(x_prompt, x_sample, p_prompt, p_sample, state_mlstm_C, state_mlstm_n, state_mlstm_m, cache_fox_k, cache_fox_v, cache_fox_lf, norm_gains, ffn1_in, ffn1_out, ffn2_in, ffn2_out, ple_gate, ple_proj, mlstm_w_in, mlstm_b_gates, mlstm_g_h, mlstm_w_out, fox_w_in, fox_b_f, fox_g_qk, fox_w_out):
    m_heads, dqk, dv = state_mlstm_C.shape[2:]
    w = {
        "m_heads": m_heads, "dqk": dqk, "dv": dv, "f_heads": cache_fox_lf.shape[-1],
        "norm_gains": norm_gains, "mlstm_b_gates": mlstm_b_gates, "mlstm_g_h": mlstm_g_h,
        "fox_b_f": fox_b_f, "fox_g_qk": fox_g_qk,
        "ffn1_in": ffn1_in, "ffn1_out": ffn1_out, "ffn2_in": ffn2_in, "ffn2_out": ffn2_out,
        "ffn_bf16": {},
        "ple_gate": ple_gate.astype(BF16), "ple_proj": ple_proj.astype(BF16),
        "mlstm_w_in": mlstm_w_in.astype(BF16), "mlstm_w_out": mlstm_w_out.astype(BF16),
        "fox_w_in": fox_w_in.astype(BF16), "fox_w_out": fox_w_out.astype(BF16),
    }
    y_s, s_c, s_n, s_m, s_k, s_v, s_lf = _run_trunk(
        x_sample, p_sample, w, (state_mlstm_C, state_mlstm_n, state_mlstm_m),
        (cache_fox_k, cache_fox_v, cache_fox_lf))
    y_p, p_c, p_n, p_m, p_k, p_v, p_lf = _run_trunk(x_prompt, p_prompt, w, None, None)
    return (y_p, y_s, p_c, p_n, p_m, p_k, p_v, p_lf, s_c, s_n, s_m, s_k, s_v, s_lf)
```

```python
import functools

import jax
import jax.numpy as jnp
from jax import lax
from jax.experimental import pallas as pl
from jax.experimental.pallas import tpu as pltpu

F32 = jnp.float32
BF16 = jnp.bfloat16
NORM_EPS = 1e-6
FFN_HALF = 0.5
V7X_VMEM_BYTES = 64 * 1024 * 1024
VMEM_LIMIT = V7X_VMEM_BYTES - 4 * 1024 * 1024
LANES = 128
SUBLANES = 8
NEG_INF = float("-inf")
LOG2E = 1.4426950408889634
ATTN_KEY_BATCH = 4


def _params(*sem):
    return pltpu.CompilerParams(dimension_semantics=sem, vmem_limit_bytes=VMEM_LIMIT)


def _tile(n, pref):
    if n <= pref:
        return n
    for t in range(pref - pref % LANES, 0, -LANES):
        if n % t == 0:
            return t
    raise ValueError((n, pref))


def _round_up(n, k):
    return (n + k - 1) // k * k


def _rms(x, g):
    ms = jnp.mean(x * x, axis=-1, keepdims=True)
    return x * lax.rsqrt(ms + NORM_EPS) * g


def _sigmoid(x):
    return 1.0 / (1.0 + jnp.exp(-x))


def _log_sigmoid(x):
    return jnp.minimum(x, 0.0) - jnp.log(1.0 + jnp.exp(-jnp.abs(x)))


def _dot(a, b):
    return jnp.dot(a, b, preferred_element_type=F32)


def _dot_nt(a, b):
    return lax.dot_general(a, b, (((1,), (1,)), ((), ())), preferred_element_type=F32)


def _dot_tn(a, b):
    return lax.dot_general(a, b, (((0,), (0,)), ((), ())), preferred_element_type=F32)


def _dot_f32(a, b):
    return jnp.dot(a, b, preferred_element_type=F32, precision=lax.Precision.HIGHEST)


def _ffn_body(x_ref, g_ref, wg_ref, wu_ref, wo_ref, out_ref, *rest, col_chunk):
    xn_ref = rest[-1]
    f = pl.program_id(1)

    @pl.when(f == 0)
    def _():
        x = x_ref[...]
        xn_ref[...] = _rms(x, g_ref[...]).astype(BF16)
        out_ref[...] = x

    if len(rest) > 1:
        wg, wu, wo = (r[...].astype(BF16) for r in (wg_ref, wu_ref, wo_ref))
        for w, cast_ref in zip((wg, wu, wo), rest[:-1]):
            cast_ref[...] = w
        wo_cols = lambda sl: wo[:, sl]
    else:
        wg, wu = wg_ref[...], wu_ref[...]
        wo_cols = lambda sl: wo_ref[:, sl]
    xn = xn_ref[...]
    g = _dot(xn, wg)
    u = _dot(xn, wu)
    h = (g * _sigmoid(g) * u * FFN_HALF).astype(BF16)
    d = out_ref.shape[1]
    for c in range(d // col_chunk):
        sl = slice(c * col_chunk, (c + 1) * col_chunk)
        out_ref[:, sl] += _dot(h, wo_cols(sl))


def _ffn(x, gain, weights, layer):
    m, d = x.shape
    cast = len(weights) == 2
    f_dim = weights[1].shape[1] if cast else weights[2].shape[0]
    tm = _tile(m, 1024)
    tf = _tile(f_dim, 512)
    nf = f_dim // tf
    x_specs = [
        pl.BlockSpec((tm, d), lambda i, f: (i, 0)),
        pl.BlockSpec((1, d), lambda i, f: (0, 0)),
    ]
    col_blk = pl.BlockSpec((d, tf), lambda i, f: (0, f))
    row_blk = pl.BlockSpec((tf, d), lambda i, f: (f, 0))
    out_shape = jax.ShapeDtypeStruct((m, d), F32)
    out_spec = pl.BlockSpec((tm, d), lambda i, f: (i, 0))
    if cast:
        assert m == tm, "the cast outputs are written once per hidden tile"
        w_in, w_out = weights
        w_specs = [
            pl.BlockSpec((None, d, tf), lambda i, f: (layer, 0, f)),
            pl.BlockSpec((None, d, tf), lambda i, f: (layer, 0, f + nf)),
            pl.BlockSpec((None, tf, d), lambda i, f: (layer, f, 0)),
        ]
        args = (w_in, w_in, w_out)
        out_shape = (out_shape, jax.ShapeDtypeStruct((d, f_dim), BF16),
                     jax.ShapeDtypeStruct((d, f_dim), BF16), jax.ShapeDtypeStruct((f_dim, d), BF16))
        out_spec = (out_spec, col_blk, col_blk, row_blk)
    else:
        w_specs = [col_blk, col_blk, row_blk]
        args = tuple(weights)
    res = pl.pallas_call(
        functools.partial(_ffn_body, col_chunk=_tile(d, 512)),
        out_shape=out_shape,
        grid=(m // tm, nf),
        in_specs=x_specs + w_specs,
        out_specs=out_spec,
        scratch_shapes=[pltpu.VMEM((tm, d), BF16)],
        compiler_params=_params("parallel", "arbitrary"),
        name="ffn_cast" if cast else "ffn",
    )(x, gain, *args)
    return (res[0], tuple(res[1:])) if cast else res


def _rows_to_cols(rows):
    ng, n = rows.shape
    padded = jnp.concatenate([rows, jnp.zeros((LANES - ng, n), F32)], axis=0)
    return jnp.transpose(padded)


def _norm_and_gates(x_ref, g_ref, wgate_ref, xn_ref, rows_ref, cols_ref, n_gates):
    xn = _rms(x_ref[...], g_ref[...]).astype(BF16)
    xn_ref[...] = xn
    cols = _dot(xn, wgate_ref[...])
    lane = lax.broadcasted_iota(jnp.int32, cols.shape, 1)
    cols = jnp.where(lane < n_gates, cols, 0.0)
    rows_ref[...] = jnp.transpose(cols)[:rows_ref.shape[0], :]
    if cols_ref is not None:
        cols_ref[...] = cols


def _mlstm_in_body(x_ref, g_ref, w_ref, wgate_ref, z_ref, rows_ref, cols_ref, xn_ref, *, n_gates):
    @pl.when(pl.program_id(1) == 0)
    def _():
        _norm_and_gates(x_ref, g_ref, wgate_ref, xn_ref, rows_ref, cols_ref, n_gates)

    z_ref[...] = _dot(xn_ref[...], w_ref[...])


def _mlstm_in_proj(x, gain, w, layer, ncols, n_gates):
    m, d = x.shape
    ng = _round_up(n_gates, SUBLANES)
    tm = _tile(m, 1024)
    tn = _tile(ncols, 1024)
    assert ncols % LANES == 0 and n_gates <= LANES
    return pl.pallas_call(
        functools.partial(_mlstm_in_body, n_gates=n_gates),
        out_shape=(jax.ShapeDtypeStruct((m, ncols), F32), jax.ShapeDtypeStruct((ng, m), F32),
                   jax.ShapeDtypeStruct((m, LANES), F32)),
        grid=(m // tm, ncols // tn),
        in_specs=[
            pl.BlockSpec((tm, d), lambda i, j: (i, 0)),
            pl.BlockSpec((1, d), lambda i, j: (0, 0)),
            pl.BlockSpec((None, d, tn), lambda i, j: (layer, 0, j)),
            pl.BlockSpec((None, d, LANES), lambda i, j: (layer, 0, ncols // LANES)),
        ],
        out_specs=(pl.BlockSpec((tm, tn), lambda i, j: (i, j)),
                   pl.BlockSpec((ng, tm), lambda i, j: (0, i)),
                   pl.BlockSpec((tm, LANES), lambda i, j: (i, 0))),
        scratch_shapes=[pltpu.VMEM((tm, d), BF16)],
        compiler_params=_params("parallel", "arbitrary"),
        name="mlstm_in_proj",
    )(x, gain, w, w)


def _fox_in_body(*refs, seg, hd, q_scale, n_gates, stacked):
    x_ref, g_ref, w_ref, wgate_ref, gq_ref, gk_ref = refs[:6]
    kprev_ref, vprev_ref = refs[6:8] if stacked else (None, None)
    q_ref, kf_ref, kb_ref, vf_ref, vb_ref, og_ref, rows_ref, xn_ref, z_ref = refs[-9:]
    j = pl.program_id(1)
    n_tiles = 4 * seg
    heads = [slice(c * hd, (c + 1) * hd) for c in range(z_ref.shape[1] // hd)]

    def finish(s):
        z = z_ref[...]
        if s == 0:
            for sl in heads:
                q_ref[:, sl] = (_rms(z[:, sl], gq_ref[...]) * q_scale).astype(BF16)
        elif s == 1:
            if stacked:
                kf_ref[:stacked] = kprev_ref[...]
            for sl in heads:
                y = _rms(z[:, sl], gk_ref[...])
                kf_ref[stacked, :, sl] = y
                kb_ref[:, sl] = y.astype(BF16)
        elif s == 2:
            if stacked:
                vf_ref[:stacked] = vprev_ref[...]
            vf_ref[stacked] = z
            vb_ref[...] = z.astype(BF16)
        else:
            og_ref[...] = z

    @pl.when(j == 0)
    def _():
        _norm_and_gates(x_ref, g_ref, wgate_ref, xn_ref, rows_ref, None, n_gates)
        z_ref[...] = _dot(xn_ref[...], w_ref[...])

    for s in range(4):
        @pl.when(jnp.logical_and(j > s * seg, j <= jnp.minimum((s + 1) * seg, n_tiles - 1)))
        def _():
            z_next = _dot(xn_ref[...], w_ref[...])
            finish(s)
            z_ref[...] = z_next

    @pl.when(j == n_tiles)
    def _():
        finish(3)


def _fox_in_proj(x, gain, w, g_qk, layer, hd, q_scale, prev_kv=None):
    m, d = x.shape
    n_gates = d // hd
    ng = _round_up(n_gates, SUBLANES)
    stacked = 0 if prev_kv is None else prev_kv[0].shape[0]
    tm = _tile(m, 1024)
    tn = _tile(d, 512)
    seg = d // tn
    col = lambda s: (lambda i, j: (i, jnp.clip(j - 1 - s * seg, 0, seg - 1)))
    seg_spec = lambda s: pl.BlockSpec((tm, tn), col(s))
    stack_spec = lambda n, s: pl.BlockSpec((n, tm, tn), lambda i, j: (0,) + col(s)(i, j))
    f32_out = jax.ShapeDtypeStruct((m, d), F32)
    bf16_out = jax.ShapeDtypeStruct((m, d), BF16)
    stack_out = jax.ShapeDtypeStruct((stacked + 1, m, d), F32)
    in_specs = [
        pl.BlockSpec((tm, d), lambda i, j: (i, 0), pipeline_mode=pl.Buffered(1)),
        pl.BlockSpec((1, d), lambda i, j: (0, 0)),
        pl.BlockSpec((None, d, tn), lambda i, j: (layer, 0, jnp.minimum(j, 4 * seg - 1))),
        pl.BlockSpec((None, d, LANES), lambda i, j: (layer, 0, 4 * d // LANES)),
        pl.BlockSpec((None, 1, hd), lambda i, j: (0, 0, 0)),
        pl.BlockSpec((None, 1, hd), lambda i, j: (1, 0, 0)),
    ]
    args = [x, gain, w, w, g_qk, g_qk]
    if stacked:
        in_specs += [stack_spec(stacked, 1), stack_spec(stacked, 2)]
        args += list(prev_kv)
    return pl.pallas_call(
        functools.partial(_fox_in_body, seg=seg, hd=hd, q_scale=q_scale, n_gates=n_gates,
                          stacked=stacked),
        out_shape=(bf16_out, stack_out, bf16_out, stack_out, bf16_out, f32_out,
                   jax.ShapeDtypeStruct((ng, m), F32)),
        grid=(m // tm, 4 * seg + 1),
        in_specs=in_specs,
        out_specs=(seg_spec(0), stack_spec(stacked + 1, 1), seg_spec(1), stack_spec(stacked + 1, 2),
                   seg_spec(2), seg_spec(3), pl.BlockSpec((ng, tm), lambda i, j: (0, i))),
        scratch_shapes=[pltpu.VMEM((tm, d), BF16), pltpu.VMEM((tm, tn), F32)],
        compiler_params=_params("parallel", "arbitrary"),
        name="fox_in_proj",
    )(*args)


def _out_body(a_ref, w_ref, x_ref, o_ref):
    o_ref[...] = x_ref[...] + _dot(a_ref[...], w_ref[...])


def _out_proj(a, w, x, layer):
    m, k = a.shape
    d = x.shape[1]
    tm = _tile(m, 1024)
    tn = _tile(d, 1024)
    return pl.pallas_call(
        _out_body,
        out_shape=jax.ShapeDtypeStruct((m, d), F32),
        grid=(m // tm, d // tn),
        in_specs=[
            pl.BlockSpec((tm, k), lambda i, j: (i, 0)),
            pl.BlockSpec((None, k, tn), lambda i, j: (layer, 0, j)),
            pl.BlockSpec((tm, tn), lambda i, j: (i, j)),
        ],
        out_specs=pl.BlockSpec((tm, tn), lambda i, j: (i, j)),
        compiler_params=_params("parallel", "arbitrary"),
        name="out_proj",
    )(a, w, x)


def _ple_body(x_ref, g_ref, wg_ref, p_ref, wp_ref, o_ref, *, col_chunk):
    x = x_ref[...]
    xn = _rms(x, g_ref[...]).astype(BF16)
    pb = p_ref[...].astype(BF16)
    for c in range(x.shape[1] // col_chunk):
        sl = slice(c * col_chunk, (c + 1) * col_chunk)
        gate = _sigmoid(_dot(xn, wg_ref[:, sl]))
        o_ref[:, sl] = x[:, sl] + gate * _dot(pb, wp_ref[:, sl])


def _ple(x, gain, w_gate, p, w_proj, layer):
    m, d = x.shape
    pd = p.shape[-1]
    tm = _tile(m, 512)
    return pl.pallas_call(
        functools.partial(_ple_body, col_chunk=_tile(d, 512)),
        out_shape=jax.ShapeDtypeStruct((m, d), F32),
        grid=(m // tm,),
        in_specs=[
            pl.BlockSpec((tm, d), lambda i: (i, 0)),
            pl.BlockSpec((1, d), lambda i: (0, 0)),
            pl.BlockSpec((None, d, d), lambda i: (layer, 0, 0), pipeline_mode=pl.Buffered(1)),
            pl.BlockSpec((None, tm, pd), lambda i: (layer, i, 0)),
            pl.BlockSpec((None, pd, d), lambda i: (layer, 0, 0), pipeline_mode=pl.Buffered(1)),
        ],
        out_specs=pl.BlockSpec((tm, d), lambda i: (i, 0)),
        compiler_params=_params("parallel"),
        name="ple",
    )(x, gain, w_gate, p, w_proj)


def _mlstm_body(q_ref, k_ref, v_ref, o_ref, gc_ref, gr_ref, bc_ref, br_ref, gh_ref,
                c0_ref, n0_ref, m0_ref, a_ref, c_ref, n_ref, m_ref, *, heads, dqk, dv):
    ci = pl.program_id(1)
    L = q_ref.shape[0]

    @pl.when(ci == 0)
    def _():
        c_ref[...] = c0_ref[...]
        n_ref[...] = n0_ref[...]
        m_ref[...] = m0_ref[...]

    row = lax.broadcasted_iota(jnp.int32, (L, L), 0)
    col = lax.broadcasted_iota(jnp.int32, (L, L), 1)
    causal = row >= col
    tril = causal.astype(F32)
    triu = (row <= col).astype(F32)

    gc = gc_ref[...] + bc_ref[...]
    gr = gr_ref[...] + br_ref[...]
    b_col = _dot_f32(tril, _log_sigmoid(gc))
    b_row = _dot_f32(_log_sigmoid(gr), triu)
    scale = dqk ** -0.5

    for h in range(heads):
        q = q_ref[:, h * dqk:(h + 1) * dqk]
        k = k_ref[:, h * dqk:(h + 1) * dqk] * scale
        v = v_ref[:, h * dv:(h + 1) * dv].astype(BF16)
        qb = q.astype(BF16)
        cst = c_ref[h]
        nst = n_ref[h]
        m_prev = m_ref[h][:, :1]

        bc = b_col[:, heads + h:heads + h + 1]
        ic = gc[:, h:h + 1]
        brow = b_row[heads + h:heads + h + 1, :]
        irow = gr[h:h + 1, :]

        dmat = jnp.where(causal, bc - brow + irow, NEG_INF)
        g = bc + m_prev
        m_t = jnp.maximum(g, jnp.max(dmat, axis=-1, keepdims=True))
        s = _dot_nt(qb, k.astype(BF16)) * jnp.exp(dmat - m_t)
        inter = jnp.exp(g - m_t)
        num = _dot(s.astype(BF16), v) + inter * _dot(qb, cst.astype(BF16))
        den = jnp.sum(s, axis=-1, keepdims=True) + inter * jnp.sum(q * nst, axis=-1, keepdims=True)
        hval = num / jnp.maximum(jnp.abs(den), jnp.exp(-m_t))

        hn = _rms(hval, gh_ref[:, h * dv:(h + 1) * dv])
        og = o_ref[:, h * dv:(h + 1) * dv]
        a_ref[:, h * dv:(h + 1) * dv] = (hn * _sigmoid(og)).astype(BF16)

        b_end = bc[L - 1:L, :]
        m_new = m_t[L - 1:L, :]
        decay = jnp.exp(b_end + m_prev - m_new)
        wa = jnp.exp(b_end - bc + ic - m_new)
        kw = k * wa
        c_ref[h] = decay * cst + _dot_tn(kw.astype(BF16), v)
        n_ref[h] = decay * nst + jnp.sum(kw, axis=0, keepdims=True)
        m_ref[h] = jnp.broadcast_to(m_new, (1, LANES))


def _mlstm(z, gates_col, gates_row, bias_lanes, bias_rows, g_h, c0, n0, m0, chunk):
    b, t, _ = z.shape
    heads, dqk, dv = c0.shape[1:]
    qk = heads * dqk
    vd = heads * dv
    assert (2 * qk) % vd == 0
    v_blk = 2 * qk // vd
    ng = gates_row.shape[1]
    nc = t // chunk
    st4 = lambda i, c: (i, 0, 0, 0)
    return pl.pallas_call(
        functools.partial(_mlstm_body, heads=heads, dqk=dqk, dv=dv),
        out_shape=(
            jax.ShapeDtypeStruct((b, t, vd), BF16),
            jax.ShapeDtypeStruct((b, heads, dqk, dv), F32),
            jax.ShapeDtypeStruct((b, heads, 1, dqk), F32),
            jax.ShapeDtypeStruct((b, heads, 1, LANES), F32),
        ),
        grid=(b, nc),
        in_specs=[
            pl.BlockSpec((None, chunk, qk), lambda i, c: (i, c, 0)),
            pl.BlockSpec((None, chunk, qk), lambda i, c: (i, c, 1)),
            pl.BlockSpec((None, chunk, vd), lambda i, c: (i, c, v_blk)),
            pl.BlockSpec((None, chunk, vd), lambda i, c: (i, c, v_blk + 1)),
            pl.BlockSpec((None, chunk, LANES), lambda i, c: (i, c, 0)),
            pl.BlockSpec((None, ng, chunk), lambda i, c: (i, 0, c)),
            pl.BlockSpec((1, LANES), lambda i, c: (0, 0)),
            pl.BlockSpec((ng, 1), lambda i, c: (0, 0)),
            pl.BlockSpec((1, vd), lambda i, c: (0, 0)),
            pl.BlockSpec((None, heads, dqk, dv), st4),
            pl.BlockSpec((None, heads, 1, dqk), st4),
            pl.BlockSpec((None, heads, 1, LANES), st4),
        ],
        out_specs=(
            pl.BlockSpec((None, chunk, vd), lambda i, c: (i, c, 0)),
            pl.BlockSpec((None, heads, dqk, dv), st4),
            pl.BlockSpec((None, heads, 1, dqk), st4),
            pl.BlockSpec((None, heads, 1, LANES), st4),
        ),
        compiler_params=_params("parallel", "arbitrary"),
        name="mlstm",
    )(z, z, z, z, gates_col, gates_row, bias_lanes, bias_rows, g_h, c0, n0, m0)


def _cumsum_body(f_ref, b_ref, lf_ref, c_ref, *col_refs, blk, activate, anchor_end, heads):
    t = f_ref.shape[1]
    row = lax.broadcasted_iota(jnp.int32, (blk, blk), 0)
    col = lax.broadcasted_iota(jnp.int32, (blk, blk), 1)
    triu = (row <= col).astype(F32)
    carry = jnp.zeros((f_ref.shape[0], 1), F32)
    for i in range(t // blk):
        sl = slice(i * blk, (i + 1) * blk)
        seg = f_ref[:, sl]
        if activate:
            seg = _log_sigmoid(seg + b_ref[...])
        lf_ref[:, sl] = seg
        cs = _dot_f32(seg, triu) + carry
        c_ref[:, sl] = cs
        carry = cs[:, blk - 1:blk]
        if col_refs and not anchor_end:
            col_refs[0][sl, :] = _rows_to_cols(seg)
            col_refs[1][sl, :] = _rows_to_cols(cs)
    if anchor_end:
        c_ref[...] = c_ref[...] - carry
        if col_refs:
            sub = lax.broadcasted_iota(jnp.int32, (SUBLANES, blk), 0)
            ones3 = jnp.where(sub < 3, 1.0, 0.0)
            for i in range(t // blk):
                sl = slice(i * blk, (i + 1) * blk)
                c = c_ref[:heads, sl]
                col_refs[0][sl, :] = _rows_to_cols(jnp.concatenate([c, c, c, ones3], axis=0))


def _forget_cumsum(f_rows, bias_col, activate, anchor_end, emit_cols, heads=None):
    b, ng, t = f_rows.shape
    blk = _tile(t, 512)
    row_spec = pl.BlockSpec((None, ng, t), lambda i: (i, 0, 0))
    row_shape = jax.ShapeDtypeStruct((b, ng, t), F32)
    out_shape, out_specs = (row_shape, row_shape), (row_spec, row_spec)
    if emit_cols:
        col_shape = jax.ShapeDtypeStruct((b, t, LANES), F32)
        col_spec = pl.BlockSpec((None, t, LANES), lambda i: (i, 0, 0))
        n_cols = 1 if anchor_end else 2
        out_shape, out_specs = out_shape + (col_shape,) * n_cols, out_specs + (col_spec,) * n_cols
    return pl.pallas_call(
        functools.partial(_cumsum_body, blk=blk, activate=activate, anchor_end=anchor_end, heads=heads),
        out_shape=out_shape,
        grid=(b,),
        in_specs=[row_spec, pl.BlockSpec((ng, 1), lambda i: (0, 0))],
        out_specs=out_specs,
        compiler_params=_params("parallel"),
        name="forget_cumsum",
    )(f_rows, bias_col)


def _split3(c):
    hi = c.astype(BF16).astype(F32)
    r = c - hi
    mid = r.astype(BF16).astype(F32)
    return hi, mid, r - mid


def _bias_lanes(c, query_side):
    hi, mid, lo = _split3(c if query_side else -c)
    lane = lax.broadcasted_iota(jnp.int32, (c.shape[0], LANES), 1)
    o = 0 if query_side else 3
    terms = jnp.where(lane == o, hi, jnp.where(lane == o + 1, mid, jnp.where(lane == o + 2, lo, 0.0)))
    ones = jnp.logical_and(lane >= 3 - o, lane < 6 - o)
    return jnp.where(ones, 1.0, terms).astype(BF16)


def _online_softmax_step(s, v, m, l, acc):
    m_new = jnp.maximum(m, jnp.max(s, axis=-1, keepdims=True))
    p = jnp.exp2(s - m_new)
    alpha = jnp.exp2(m - m_new)
    l = alpha * l + jnp.sum(p, axis=-1, keepdims=True)
    acc = alpha * acc + _dot(p.astype(BF16), v)
    return m_new, l, acc


def _bias_rows(c, rows):
    hi, mid, lo = _split3(c)
    sub = lax.broadcasted_iota(jnp.int32, (rows, c.shape[1]), 0)
    terms = jnp.where(sub == 0, hi, jnp.where(sub == 1, mid, jnp.where(sub == 2, lo, 0.0)))
    return jnp.where(jnp.logical_and(sub >= 3, sub < 6), 1.0, terms).astype(BF16)


def _fox_prompt_body(q_ref, k_ref, v_ref, og_ref, cc_ref, cr_ref, a_ref, qt_ref, ka_ref, vt_ref, *, tq,
                     kbatch):
    h = pl.program_id(1)
    t, hd = q_ref.shape
    nt = t // tq
    for r in range(nt):
        rows = slice(r * tq, (r + 1) * tq)
        cc = cc_ref[rows, :]
        lane = lax.broadcasted_iota(jnp.int32, cc.shape, 1)
        c_col = jnp.sum(jnp.where(lane == h, cc, 0.0), axis=-1, keepdims=True) * LOG2E
        ka_ref[rows, :hd] = k_ref[rows, :]
        ka_ref[rows, hd:] = _bias_lanes(c_col, False)
        qt_ref[:hd, rows] = jnp.transpose(q_ref[rows, :].astype(F32)).astype(BF16)
        qt_ref[hd:, rows] = _bias_rows(cr_ref[r:r + 1, :] * LOG2E, LANES)
        vt_ref[:, rows] = jnp.transpose(v_ref[rows, :].astype(F32)).astype(BF16)

    key = lax.broadcasted_iota(jnp.int32, (tq, tq), 0)
    qry = lax.broadcasted_iota(jnp.int32, (tq, tq), 1)
    causal = key <= qry

    def tile_off(kj):
        return kj * tq if isinstance(kj, int) else pl.multiple_of(kj * tq, tq)

    for qi in range(nt):
        cols = slice(qi * tq, (qi + 1) * tq)
        qt = qt_ref[:, cols]

        def steps(carry, tiles):
            m, l, acc = carry
            scores = [_dot(ka_ref[pl.ds(tile_off(kj), tq), :], qt) for kj, _ in tiles]
            for s, (kj, diagonal) in zip(scores, tiles):
                if diagonal:
                    s = jnp.where(causal, s, NEG_INF)
                m_new = jnp.maximum(m, jnp.max(s, axis=0, keepdims=True))
                p = jnp.exp2(s - m_new)
                alpha = jnp.exp2(m - m_new)
                l = alpha * l + jnp.sum(p, axis=0, keepdims=True)
                acc = alpha * acc + _dot(vt_ref[:, pl.ds(tile_off(kj), tq)], p.astype(BF16))
                m = m_new
            return m, l, acc

        carry = (jnp.full((1, tq), NEG_INF, F32), jnp.zeros((1, tq), F32), jnp.zeros((hd, tq), F32))
        nb = qi // kbatch
        carry = lax.fori_loop(
            0, nb, lambda i, c: steps(c, [(i * kbatch + u, False) for u in range(kbatch)]), carry)
        tail = [(kj, False) for kj in range(nb * kbatch, qi)] + [(qi, True)]
        _, l, acc = steps(carry, tail)
        o = jnp.transpose(acc / l)
        a_ref[cols, :] = (o * _sigmoid(og_ref[cols, :])).astype(BF16)


def _fox_prompt_attn(qb, kb, vb, og, c_col, c_row, heads):
    b, t, d = qb.shape
    hd = d // heads
    tq = _tile(t, 512)
    nt = t // tq
    c_row4 = c_row.reshape(b, c_row.shape[1], nt, tq)
    head_blk = pl.BlockSpec((None, t, hd), lambda i, h: (i, 0, h))
    return pl.pallas_call(
        functools.partial(_fox_prompt_body, tq=tq, kbatch=ATTN_KEY_BATCH),
        out_shape=jax.ShapeDtypeStruct((b, t, d), BF16),
        grid=(b, heads),
        in_specs=[
            head_blk, head_blk, head_blk, head_blk,
            pl.BlockSpec((None, t, LANES), lambda i, h: (i, 0, 0)),
            pl.BlockSpec((None, None, nt, tq), lambda i, h: (i, h, 0, 0)),
        ],
        out_specs=head_blk,
        scratch_shapes=[pltpu.VMEM((hd + LANES, t), BF16), pltpu.VMEM((t, hd + LANES), BF16),
                        pltpu.VMEM((hd, t), BF16)],
        compiler_params=_params("parallel", "arbitrary"),
        name="fox_prompt_attn",
    )(qb, kb, vb, og, c_col, c_row4)


def _key_bias_lanes(c3, heads):
    lane = lax.broadcasted_iota(jnp.int32, c3.shape, 1)
    hi, mid, lo = _split3(c3 * jnp.where(lane < 3 * heads, LOG2E, 1.0))
    pick = jnp.where(lane < heads, hi, jnp.where(lane < 2 * heads, mid, jnp.where(lane < 3 * heads, lo, hi)))
    return pick.astype(BF16)


def _fox_sample_body(q_ref, kp_ref, vp_ref, kn_ref, vn_ref, og_ref, cq_ref, c3p_ref, c3n_ref,
                     a_ref, wq_ref, m_ref, l_ref, acc_ref, *, heads, hd, hg):
    j = pl.program_id(1)
    nj = pl.num_programs(1)
    tq = q_ref.shape[0]
    groups = heads // hg
    tk = kp_ref.shape[0] // heads

    @pl.when(j == 0)
    def _():
        m_ref[...] = jnp.full(m_ref.shape, NEG_INF, F32)
        l_ref[...] = jnp.zeros(l_ref.shape, F32)
        acc_ref[...] = jnp.zeros(acc_ref.shape, F32)
        wq_ref[...] = jnp.zeros(wq_ref.shape, BF16)
        lane = lax.broadcasted_iota(jnp.int32, (tq, LANES), 1)
        for g in range(groups):
            for i in range(hg):
                h = g * hg + i
                rows = slice(i * tq, (i + 1) * tq)
                wq_ref[g, rows, i * hd:(i + 1) * hd] = q_ref[:, h * hd:(h + 1) * hd]
                hi, mid, lo = _split3(cq_ref[:, h:h + 1] * LOG2E)
                own = jnp.logical_or(lane == h, jnp.logical_or(lane == heads + h, lane == 2 * heads + h))
                cq3 = jnp.where(lane == 3 * heads, hi,
                                jnp.where(lane == 3 * heads + 1, mid,
                                          jnp.where(lane == 3 * heads + 2, lo, 0.0)))
                wq_ref[g, rows, hg * hd:] = jnp.where(own, -1.0, cq3).astype(BF16)

    def update(g, s, v):
        m, l, acc = _online_softmax_step(s, v, m_ref[g], l_ref[g], acc_ref[g])
        m_ref[g] = m
        l_ref[g] = l
        acc_ref[g] = acc

    kbias = _key_bias_lanes(c3p_ref[...], heads)
    for g in range(groups):
        head_rows = lambda i: pl.ds(g * hg + i, tk, stride=heads)
        ks = [kp_ref[head_rows(i), :].astype(BF16) for i in range(hg)]
        vs = [vp_ref[head_rows(i), :].astype(BF16) for i in range(hg)]
        s = _dot_nt(wq_ref[g], jnp.concatenate(ks + [kbias], axis=1))
        update(g, s, jnp.concatenate(vs, axis=1))

    @pl.when(j == nj - 1)
    def _():
        row = lax.broadcasted_iota(jnp.int32, (hg * tq, tq), 0)
        col = lax.broadcasted_iota(jnp.int32, (hg * tq, tq), 1)
        causal = row % tq >= col
        nbias = _key_bias_lanes(c3n_ref[...], heads)
        for g in range(groups):
            cols = slice(g * hg * hd, (g + 1) * hg * hd)
            s = _dot_nt(wq_ref[g], jnp.concatenate([kn_ref[:, cols], nbias], axis=1))
            update(g, jnp.where(causal, s, NEG_INF), vn_ref[:, cols])
            for i in range(hg):
                rows = slice(i * tq, (i + 1) * tq)
                sl = slice((g * hg + i) * hd, (g * hg + i + 1) * hd)
                o = acc_ref[g, rows, i * hd:(i + 1) * hd] / l_ref[g, rows, :]
                a_ref[:, sl] = (o * _sigmoid(og_ref[:, sl])).astype(BF16)


def _fox_sample_attn(qb, k_past, v_past, layer, kb, vb, og, cq_col, c3_past, c3_new):
    b, tq, d = qb.shape
    p, heads, hd = k_past.shape[2:]
    assert 3 * heads + 3 <= LANES
    hg = max(1, min(heads, 512 // hd))
    assert heads % hg == 0
    tk = _tile(p, 1024)
    new_blk = pl.BlockSpec((None, tq, d), lambda i, j: (i, 0, 0))
    past_blk = pl.BlockSpec((None, None, tk * heads, hd), lambda i, j: (layer, i, j, 0))
    flat = lambda a: a.reshape(a.shape[:2] + (p * heads, hd))
    return pl.pallas_call(
        functools.partial(_fox_sample_body, heads=heads, hd=hd, hg=hg),
        out_shape=jax.ShapeDtypeStruct((b, tq, d), BF16),
        grid=(b, p // tk),
        in_specs=[
            new_blk, past_blk, past_blk, new_blk, new_blk, new_blk,
            pl.BlockSpec((None, tq, LANES), lambda i, j: (i, 0, 0)),
            pl.BlockSpec((None, tk, LANES), lambda i, j: (i, j, 0)),
            pl.BlockSpec((None, tq, LANES), lambda i, j: (i, 0, 0)),
        ],
        out_specs=new_blk,
        scratch_shapes=[
            pltpu.VMEM((heads // hg, hg * tq, hg * hd + LANES), BF16),
            pltpu.VMEM((heads // hg, hg * tq, 1), F32),
            pltpu.VMEM((heads // hg, hg * tq, 1), F32),
            pltpu.VMEM((heads // hg, hg * tq, hg * hd), F32),
        ],
        compiler_params=_params("parallel", "arbitrary"),
        name="fox_sample_attn",
    )(qb, flat(k_past), flat(v_past), kb, vb, og, cq_col, c3_past, c3_new)


def _pad_rows(a, rows):
    return jnp.pad(a, ((0, rows - a.shape[0]),) + ((0, 0),) * (a.ndim - 1))


def _col_layout(rows):
    c = jnp.swapaxes(rows, 1, 2)
    return jnp.pad(c, ((0, 0), (0, 0), (0, LANES - c.shape[-1])))


def _c3_layout(rows, heads):
    c = jnp.swapaxes(rows[:, :heads, :], 1, 2)
    c3 = jnp.concatenate([c, c, c, jnp.ones(c.shape[:2] + (3,), F32)], axis=-1)
    return jnp.pad(c3, ((0, 0), (0, 0), (0, LANES - c3.shape[-1])))


def _run_trunk(x, p, w, mlstm_init, fox_past):
    b, t, d = x.shape
    depth = w["norm_gains"].shape[0]
    m_heads, dqk, dv = w["m_heads"], w["dqk"], w["dv"]
    f_heads = w["f_heads"]
    hd = d // f_heads
    qk, vd = m_heads * dqk, m_heads * dv
    mtok = b * t
    xf = x.reshape(mtok, d)
    pf = p.reshape(depth, mtok, p.shape[-1])
    chunk = _tile(t, 256)
    m_states, f_rows, kv_stacks = [], [], None

    def ffn(xf, g, name, i):
        key = (name, i)
        if key in w["ffn_bf16"]:
            return _ffn(xf, g, w["ffn_bf16"][key], i)
        out, w["ffn_bf16"][key] = _ffn(xf, g, (w[name + "_in"], w[name + "_out"]), i)
        return out

    for i in range(depth):
        gains = w["norm_gains"][i]
        gain = lambda r: gains[r].reshape(1, d)
        j = i // 2
        xf = ffn(xf, gain(0), "ffn1", i)
        if i % 2 == 0:
            z, g_rows, g_cols = _mlstm_in_proj(xf, gain(1), w["mlstm_w_in"], j, 2 * qk + 2 * vd,
                                               2 * m_heads)
            ng = g_rows.shape[0]
            g_rows = jnp.swapaxes(g_rows.reshape(ng, b, t), 0, 1)
            if mlstm_init is None:
                c0 = jnp.zeros((b, m_heads, dqk, dv), F32)
                n0 = jnp.zeros((b, m_heads, dqk), F32)
                m0 = jnp.zeros((b, m_heads), F32)
            else:
                c0, n0, m0 = mlstm_init[0][j], mlstm_init[1][j], mlstm_init[2][j]
            bias = w["mlstm_b_gates"][j].reshape(2 * m_heads)
            a, c_new, n_new, m_new = _mlstm(
                z.reshape(b, t, -1), g_cols.reshape(b, t, LANES), g_rows,
                jnp.pad(bias, (0, LANES - 2 * m_heads)).reshape(1, LANES),
                _pad_rows(bias.reshape(-1, 1), ng),
                w["mlstm_g_h"][j].reshape(1, vd),
                c0, n0.reshape(b, m_heads, 1, dqk),
                jnp.broadcast_to(m0[:, :, None, None], (b, m_heads, 1, LANES)), chunk)
            m_states.append((c_new, n_new.reshape(b, m_heads, dqk), m_new[:, :, 0, 0]))
            xf = _out_proj(a.reshape(mtok, vd), w["mlstm_w_out"], xf, j)
        else:
            qb, k_stack, kb, v_stack, vb, og, f_pre = _fox_in_proj(
                xf, gain(1), w["fox_w_in"], w["fox_g_qk"][j].reshape(2, 1, hd), j, hd,
                hd ** -0.5 * LOG2E, kv_stacks)
            kv_stacks = (k_stack, v_stack)
            ng = f_pre.shape[0]
            f_pre = jnp.swapaxes(f_pre.reshape(ng, b, t), 0, 1)
            bias_col = _pad_rows(w["fox_b_f"][j].reshape(-1, 1), ng)
            shp = (b, t, d)
            if fox_past is None:
                lf_row, c_row, lf_col, c_col = _forget_cumsum(f_pre, bias_col, True, False, True)
                a = _fox_prompt_attn(qb.reshape(shp), kb.reshape(shp), vb.reshape(shp),
                                     og.reshape(shp), c_col, c_row, f_heads)
                lf = lf_col[:, :, :f_heads]
            else:
                lf_row, c_row = _forget_cumsum(f_pre, bias_col, True, False, False)
                lf_past_row = jnp.swapaxes(fox_past[2][j].astype(F32), 1, 2)
                lf_past_row = jnp.pad(lf_past_row, ((0, 0), (0, ng - f_heads), (0, 0)))
                _, _, c3_past = _forget_cumsum(lf_past_row, bias_col, False, True, True, f_heads)
                a = _fox_sample_attn(qb.reshape(shp), fox_past[0], fox_past[1], j,
                                     kb.reshape(shp), vb.reshape(shp), og.reshape(shp),
                                     _col_layout(c_row), c3_past, _c3_layout(c_row, f_heads))
                lf = jnp.swapaxes(lf_row[:, :f_heads, :], 1, 2)
            f_rows.append(lf)
            xf = _out_proj(a.reshape(mtok, d), w["fox_w_out"], xf, j)
        xf = ffn(xf, gain(2), "ffn2", i)
        xf = _ple(xf, gain(3), w["ple_gate"], pf, w["ple_proj"], i)
    stack = lambda k: jnp.stack([s[k] for s in m_states])
    k_all, v_all = (a.reshape(a.shape[0], b, t, f_heads, hd) for a in kv_stacks)
    return (xf.reshape(b, t, d), stack(0), stack(1), stack(2), k_all, v_all, jnp.stack(f_rows))


def kernel(x_prompt, x_sample, p_prompt, p_sample, state_mlstm_C, state_mlstm_n, state_mlstm_m, cache_fox_k, cache_fox_v, cache_fox_lf, norm_gains, ffn1_in, ffn1_out, ffn2_in, ffn2_out, ple_gate, ple_proj, mlstm_w_in, mlstm_b_gates, mlstm_g_h, mlstm_w_out, fox_w_in, fox_b_f, fox_g_qk, fox_w_out):
    m_heads, dqk, dv = state_mlstm_C.shape[2:]
    w = {
        "m_heads": m_heads, "dqk": dqk, "dv": dv, "f_heads": cache_fox_lf.shape[-1],
        "norm_gains": norm_gains, "mlstm_b_gates": mlstm_b_gates, "mlstm_g_h": mlstm_g_h,
        "fox_b_f": fox_b_f, "fox_g_qk": fox_g_qk,
        "ffn1_in": ffn1_in, "ffn1_out": ffn1_out, "ffn2_in": ffn2_in, "ffn2_out": ffn2_out,
        "ffn_bf16": {},
        "ple_gate": ple_gate.astype(BF16), "ple_proj": ple_proj.astype(BF16),
        "mlstm_w_in": mlstm_w_in.astype(BF16), "mlstm_w_out": mlstm_w_out.astype(BF16),
        "fox_w_in": fox_w_in.astype(BF16), "fox_w_out": fox_w_out.astype(BF16),
    }
    y_s, s_c, s_n, s_m, s_k, s_v, s_lf = _run_trunk(
        x_sample, p_sample, w, (state_mlstm_C, state_mlstm_n, state_mlstm_m),
        (cache_fox_k, cache_fox_v, cache_fox_lf))
    y_p, p_c, p_n, p_m, p_k, p_v, p_lf = _run_trunk(x_prompt, p_prompt, w, None, None)
    return (y_p, y_s, p_c, p_n, p_m, p_k, p_v, p_lf, s_c, s_n, s_m, s_k, s_v, s_lf)
```

```python
import functools

import jax
import jax.numpy as jnp
from jax import lax
from jax.experimental import pallas as pl
from jax.experimental.pallas import tpu as pltpu

F32 = jnp.float32
BF16 = jnp.bfloat16
NORM_EPS = 1e-6
FFN_HALF = 0.5
V7X_VMEM_BYTES = 64 * 1024 * 1024
VMEM_LIMIT = V7X_VMEM_BYTES - 4 * 1024 * 1024
LANES = 128
SUBLANES = 8
NEG_INF = float("-inf")
LOG2E = 1.4426950408889634
ATTN_KEY_BATCH = 4


def _params(*sem):
    return pltpu.CompilerParams(dimension_semantics=sem, vmem_limit_bytes=VMEM_LIMIT)


def _tile(n, pref):
    if n <= pref:
        return n
    for t in range(pref - pref % LANES, 0, -LANES):
        if n % t == 0:
            return t
    raise ValueError((n, pref))


def _round_up(n, k):
    return (n + k - 1) // k * k


def _rms(x, g):
    ms = jnp.mean(x * x, axis=-1, keepdims=True)
    return x * lax.rsqrt(ms + NORM_EPS) * g


def _sigmoid(x):
    return 1.0 / (1.0 + jnp.exp(-x))


def _log_sigmoid(x):
    return jnp.minimum(x, 0.0) - jnp.log(1.0 + jnp.exp(-jnp.abs(x)))


def _dot(a, b):
    return jnp.dot(a, b, preferred_element_type=F32)


def _dot_nt(a, b):
    return lax.dot_general(a, b, (((1,), (1,)), ((), ())), preferred_element_type=F32)


def _dot_tn(a, b):
    return lax.dot_general(a, b, (((0,), (0,)), ((), ())), preferred_element_type=F32)


def _dot_f32(a, b):
    return jnp.dot(a, b, preferred_element_type=F32, precision=lax.Precision.HIGHEST)


def _ffn_body(x_ref, g_ref, wg_ref, wu_ref, wo_ref, out_ref, *rest, col_chunk):
    xn_ref = rest[-1]
    f = pl.program_id(1)

    @pl.when(f == 0)
    def _():
        x = x_ref[...]
        xn_ref[...] = _rms(x, g_ref[...]).astype(BF16)
        out_ref[...] = x

    if len(rest) > 1:
        wg, wu, wo = (r[...].astype(BF16) for r in (wg_ref, wu_ref, wo_ref))
        for w, cast_ref in zip((wg, wu, wo), rest[:-1]):
            cast_ref[...] = w
        wo_cols = lambda sl: wo[:, sl]
    else:
        wg, wu = wg_ref[...], wu_ref[...]
        wo_cols = lambda sl: wo_ref[:, sl]
    xn = xn_ref[...]
    g = _dot(xn, wg)
    u = _dot(xn, wu)
    h = (g * _sigmoid(g) * u * FFN_HALF).astype(BF16)
    d = out_ref.shape[1]
    for c in range(d // col_chunk):
        sl = slice(c * col_chunk, (c + 1) * col_chunk)
        out_ref[:, sl] += _dot(h, wo_cols(sl))


def _ffn(x, gain, weights, layer):
    m, d = x.shape
    cast = len(weights) == 2
    f_dim = weights[1].shape[1] if cast else weights[2].shape[0]
    tm = _tile(m, 1024)
    tf = _tile(f_dim, 512)
    nf = f_dim // tf
    x_specs = [
        pl.BlockSpec((tm, d), lambda i, f: (i, 0)),
        pl.BlockSpec((1, d), lambda i, f: (0, 0)),
    ]
    col_blk = pl.BlockSpec((d, tf), lambda i, f: (0, f))
    row_blk = pl.BlockSpec((tf, d), lambda i, f: (f, 0))
    out_shape = jax.ShapeDtypeStruct((m, d), F32)
    out_spec = pl.BlockSpec((tm, d), lambda i, f: (i, 0))
    if cast:
        assert m == tm, "the cast outputs are written once per hidden tile"
        w_in, w_out = weights
        w_specs = [
            pl.BlockSpec((None, d, tf), lambda i, f: (layer, 0, f)),
            pl.BlockSpec((None, d, tf), lambda i, f: (layer, 0, f + nf)),
            pl.BlockSpec((None, tf, d), lambda i, f: (layer, f, 0)),
        ]
        args = (w_in, w_in, w_out)
        out_shape = (out_shape, jax.ShapeDtypeStruct((d, f_dim), BF16),
                     jax.ShapeDtypeStruct((d, f_dim), BF16), jax.ShapeDtypeStruct((f_dim, d), BF16))
        out_spec = (out_spec, col_blk, col_blk, row_blk)
    else:
        w_specs = [col_blk, col_blk, row_blk]
        args = tuple(weights)
    res = pl.pallas_call(
        functools.partial(_ffn_body, col_chunk=_tile(d, 512)),
        out_shape=out_shape,
        grid=(m // tm, nf),
        in_specs=x_specs + w_specs,
        out_specs=out_spec,
        scratch_shapes=[pltpu.VMEM((tm, d), BF16)],
        compiler_params=_params("parallel", "arbitrary"),
        name="ffn_cast" if cast else "ffn",
    )(x, gain, *args)
    return (res[0], tuple(res[1:])) if cast else res


def _rows_to_cols(rows):
    ng, n = rows.shape
    padded = jnp.concatenate([rows, jnp.zeros((LANES - ng, n), F32)], axis=0)
    return jnp.transpose(padded)


def _load_weight(w_ref, cast_ref):
    w = w_ref[...].astype(BF16)
    if cast_ref is not None:
        cast_ref[...] = w
    return w


def _norm_and_gates(x_ref, g_ref, wgate_ref, xn_ref, rows_ref, cols_ref, n_gates, gate_cast_ref):
    xn = _rms(x_ref[...], g_ref[...]).astype(BF16)
    xn_ref[...] = xn
    wgate = wgate_ref[...].astype(BF16)
    sub = lax.broadcasted_iota(jnp.int32, wgate.shape, 0)
    wgate = jnp.where(sub < n_gates, wgate, jnp.zeros_like(wgate))
    if gate_cast_ref is not None:
        gate_cast_ref[...] = wgate
    cols = _dot_nt(xn, wgate)
    rows_ref[...] = jnp.transpose(cols)[:rows_ref.shape[0], :]
    if cols_ref is not None:
        cols_ref[...] = cols


def _mlstm_in_body(x_ref, g_ref, w_ref, wgate_ref, z_ref, rows_ref, cols_ref, *rest, n_gates):
    xn_ref = rest[-1]
    w_cast_ref, gate_cast_ref = rest[:2] if len(rest) > 1 else (None, None)

    @pl.when(pl.program_id(1) == 0)
    def _():
        _norm_and_gates(x_ref, g_ref, wgate_ref, xn_ref, rows_ref, cols_ref, n_gates, gate_cast_ref)

    z_ref[...] = _dot_nt(xn_ref[...], _load_weight(w_ref, w_cast_ref))


def _in_proj_weight_specs(w, layer, ncols, tn, d, tile_index):
    if len(w) == 1:
        args = (w[0], w[0])
        specs = [pl.BlockSpec((None, tn, d), lambda i, j: (layer, tile_index(j), 0)),
                 pl.BlockSpec((None, LANES, d), lambda i, j: (layer, ncols // LANES, 0))]
        shapes = (jax.ShapeDtypeStruct((ncols, d), BF16), jax.ShapeDtypeStruct((LANES, d), BF16))
        cast_specs = (pl.BlockSpec((tn, d), lambda i, j: (tile_index(j), 0)),
                      pl.BlockSpec((LANES, d), lambda i, j: (0, 0)))
        return args, specs, shapes, cast_specs
    specs = [pl.BlockSpec((tn, d), lambda i, j: (tile_index(j), 0)),
             pl.BlockSpec((LANES, d), lambda i, j: (0, 0))]
    return tuple(w), specs, (), ()


def _mlstm_in_proj(x, gain, w, layer, ncols, n_gates):
    m, d = x.shape
    ng = _round_up(n_gates, SUBLANES)
    tm = _tile(m, 1024)
    tn = _tile(ncols, 1024)
    assert ncols % LANES == 0 and n_gates <= LANES
    w_args, w_specs, cast_shapes, cast_specs = _in_proj_weight_specs(w, layer, ncols, tn, d, lambda j: j)
    assert not cast_shapes or m == tm, "the casts are written once per column tile"
    res = pl.pallas_call(
        functools.partial(_mlstm_in_body, n_gates=n_gates),
        out_shape=(jax.ShapeDtypeStruct((m, ncols), F32), jax.ShapeDtypeStruct((ng, m), F32),
                   jax.ShapeDtypeStruct((m, LANES), F32)) + cast_shapes,
        grid=(m // tm, ncols // tn),
        in_specs=[
            pl.BlockSpec((tm, d), lambda i, j: (i, 0)),
            pl.BlockSpec((1, d), lambda i, j: (0, 0)),
        ] + w_specs,
        out_specs=(pl.BlockSpec((tm, tn), lambda i, j: (i, j)),
                   pl.BlockSpec((ng, tm), lambda i, j: (0, i)),
                   pl.BlockSpec((tm, LANES), lambda i, j: (i, 0))) + cast_specs,
        scratch_shapes=[pltpu.VMEM((tm, d), BF16)],
        compiler_params=_params("parallel", "arbitrary"),
        name="mlstm_in_proj",
    )(x, gain, *w_args)
    return res[:3], tuple(res[3:])


def _fox_in_body(*refs, seg, hd, q_scale, n_gates, stacked):
    x_ref, g_ref, w_ref, wgate_ref, gq_ref, gk_ref = refs[:6]
    kprev_ref, vprev_ref = refs[6:8] if stacked else (None, None)
    outs = refs[6 + (2 if stacked else 0):-2]
    q_ref, kf_ref, kb_ref, vf_ref, vb_ref, og_ref, rows_ref = outs[:7]
    w_cast_ref, gate_cast_ref = outs[7:] if len(outs) > 7 else (None, None)
    xn_ref, z_ref = refs[-2:]
    j = pl.program_id(1)
    n_tiles = 4 * seg
    heads = [slice(c * hd, (c + 1) * hd) for c in range(z_ref.shape[1] // hd)]

    def finish(s):
        z = z_ref[...]
        if s == 0:
            for sl in heads:
                q_ref[:, sl] = (_rms(z[:, sl], gq_ref[...]) * q_scale).astype(BF16)
        elif s == 1:
            if stacked:
                kf_ref[:stacked] = kprev_ref[...]
            for sl in heads:
                y = _rms(z[:, sl], gk_ref[...])
                kf_ref[stacked, :, sl] = y
                kb_ref[:, sl] = y.astype(BF16)
        elif s == 2:
            if stacked:
                vf_ref[:stacked] = vprev_ref[...]
            vf_ref[stacked] = z
            vb_ref[...] = z.astype(BF16)
        else:
            og_ref[...] = z

    @pl.when(j == 0)
    def _():
        _norm_and_gates(x_ref, g_ref, wgate_ref, xn_ref, rows_ref, None, n_gates, gate_cast_ref)
        z_ref[...] = _dot_nt(xn_ref[...], _load_weight(w_ref, w_cast_ref))

    for s in range(4):
        @pl.when(jnp.logical_and(j > s * seg, j <= jnp.minimum((s + 1) * seg, n_tiles - 1)))
        def _():
            z_next = _dot_nt(xn_ref[...], _load_weight(w_ref, w_cast_ref))
            finish(s)
            z_ref[...] = z_next

    @pl.when(j == n_tiles)
    def _():
        finish(3)


def _fox_in_proj(x, gain, w, g_qk, layer, hd, q_scale, prev_kv=None):
    m, d = x.shape
    n_gates = d // hd
    ng = _round_up(n_gates, SUBLANES)
    stacked = 0 if prev_kv is None else prev_kv[0].shape[0]
    tm = _tile(m, 1024)
    tn = _tile(d, 512)
    seg = d // tn
    col = lambda s: (lambda i, j: (i, jnp.clip(j - 1 - s * seg, 0, seg - 1)))
    seg_spec = lambda s: pl.BlockSpec((tm, tn), col(s))
    stack_spec = lambda n, s: pl.BlockSpec((n, tm, tn), lambda i, j: (0,) + col(s)(i, j))
    f32_out = jax.ShapeDtypeStruct((m, d), F32)
    bf16_out = jax.ShapeDtypeStruct((m, d), BF16)
    stack_out = jax.ShapeDtypeStruct((stacked + 1, m, d), F32)
    w_args, w_specs, cast_shapes, cast_specs = _in_proj_weight_specs(
        w, layer, 4 * d, tn, d, lambda j: jnp.minimum(j, 4 * seg - 1))
    assert not cast_shapes or m == tm, "the casts are written once per column tile"
    in_specs = [
        pl.BlockSpec((tm, d), lambda i, j: (i, 0), pipeline_mode=pl.Buffered(1)),
        pl.BlockSpec((1, d), lambda i, j: (0, 0)),
    ] + w_specs + [
        pl.BlockSpec((None, 1, hd), lambda i, j: (0, 0, 0)),
        pl.BlockSpec((None, 1, hd), lambda i, j: (1, 0, 0)),
    ]
    args = [x, gain, *w_args, g_qk, g_qk]
    if stacked:
        in_specs += [stack_spec(stacked, 1), stack_spec(stacked, 2)]
        args += list(prev_kv)
    res = pl.pallas_call(
        functools.partial(_fox_in_body, seg=seg, hd=hd, q_scale=q_scale, n_gates=n_gates,
                          stacked=stacked),
        out_shape=(bf16_out, stack_out, bf16_out, stack_out, bf16_out, f32_out,
                   jax.ShapeDtypeStruct((ng, m), F32)) + cast_shapes,
        grid=(m // tm, 4 * seg + 1),
        in_specs=in_specs,
        out_specs=(seg_spec(0), stack_spec(stacked + 1, 1), seg_spec(1), stack_spec(stacked + 1, 2),
                   seg_spec(2), seg_spec(3), pl.BlockSpec((ng, tm), lambda i, j: (0, i))) + cast_specs,
        scratch_shapes=[pltpu.VMEM((tm, d), BF16), pltpu.VMEM((tm, tn), F32)],
        compiler_params=_params("parallel", "arbitrary"),
        name="fox_in_proj",
    )(*args)
    return res[:7], tuple(res[7:])


def _out_body(a_ref, w_ref, x_ref, o_ref):
    o_ref[...] = x_ref[...] + _dot(a_ref[...], w_ref[...])


def _out_proj(a, w, x, layer):
    m, k = a.shape
    d = x.shape[1]
    tm = _tile(m, 1024)
    tn = _tile(d, 1024)
    return pl.pallas_call(
        _out_body,
        out_shape=jax.ShapeDtypeStruct((m, d), F32),
        grid=(m // tm, d // tn),
        in_specs=[
            pl.BlockSpec((tm, k), lambda i, j: (i, 0)),
            pl.BlockSpec((None, k, tn), lambda i, j: (layer, 0, j)),
            pl.BlockSpec((tm, tn), lambda i, j: (i, j)),
        ],
        out_specs=pl.BlockSpec((tm, tn), lambda i, j: (i, j)),
        compiler_params=_params("parallel", "arbitrary"),
        name="out_proj",
    )(a, w, x)


def _ple_body(x_ref, g_ref, wg_ref, p_ref, wp_ref, o_ref, *, col_chunk):
    x = x_ref[...]
    xn = _rms(x, g_ref[...]).astype(BF16)
    pb = p_ref[...].astype(BF16)
    for c in range(x.shape[1] // col_chunk):
        sl = slice(c * col_chunk, (c + 1) * col_chunk)
        gate = _sigmoid(_dot(xn, wg_ref[:, sl]))
        o_ref[:, sl] = x[:, sl] + gate * _dot(pb, wp_ref[:, sl])


def _ple(x, gain, w_gate, p, w_proj, layer):
    m, d = x.shape
    pd = p.shape[-1]
    tm = _tile(m, 512)
    return pl.pallas_call(
        functools.partial(_ple_body, col_chunk=_tile(d, 512)),
        out_shape=jax.ShapeDtypeStruct((m, d), F32),
        grid=(m // tm,),
        in_specs=[
            pl.BlockSpec((tm, d), lambda i: (i, 0)),
            pl.BlockSpec((1, d), lambda i: (0, 0)),
            pl.BlockSpec((None, d, d), lambda i: (layer, 0, 0), pipeline_mode=pl.Buffered(1)),
            pl.BlockSpec((None, tm, pd), lambda i: (layer, i, 0)),
            pl.BlockSpec((None, pd, d), lambda i: (layer, 0, 0), pipeline_mode=pl.Buffered(1)),
        ],
        out_specs=pl.BlockSpec((tm, d), lambda i: (i, 0)),
        compiler_params=_params("parallel"),
        name="ple",
    )(x, gain, w_gate, p, w_proj)


def _mlstm_body(q_ref, k_ref, v_ref, o_ref, gc_ref, gr_ref, bc_ref, br_ref, gh_ref,
                c0_ref, n0_ref, m0_ref, a_ref, c_ref, n_ref, m_ref, *, heads, dqk, dv):
    ci = pl.program_id(1)
    L = q_ref.shape[0]

    @pl.when(ci == 0)
    def _():
        c_ref[...] = c0_ref[...]
        n_ref[...] = n0_ref[...]
        m_ref[...] = m0_ref[...]

    row = lax.broadcasted_iota(jnp.int32, (L, L), 0)
    col = lax.broadcasted_iota(jnp.int32, (L, L), 1)
    causal = row >= col
    tril = causal.astype(F32)
    triu = (row <= col).astype(F32)

    gc = gc_ref[...] + bc_ref[...]
    gr = gr_ref[...] + br_ref[...]
    b_col = _dot_f32(tril, _log_sigmoid(gc))
    b_row = _dot_f32(_log_sigmoid(gr), triu)
    scale = dqk ** -0.5

    for h in range(heads):
        q = q_ref[:, h * dqk:(h + 1) * dqk]
        k = k_ref[:, h * dqk:(h + 1) * dqk] * scale
        v = v_ref[:, h * dv:(h + 1) * dv].astype(BF16)
        qb = q.astype(BF16)
        cst = c_ref[h]
        nst = n_ref[h]
        m_prev = m_ref[h][:, :1]

        bc = b_col[:, heads + h:heads + h + 1]
        ic = gc[:, h:h + 1]
        brow = b_row[heads + h:heads + h + 1, :]
        irow = gr[h:h + 1, :]

        dmat = jnp.where(causal, bc - brow + irow, NEG_INF)
        g = bc + m_prev
        m_t = jnp.maximum(g, jnp.max(dmat, axis=-1, keepdims=True))
        s = _dot_nt(qb, k.astype(BF16)) * jnp.exp(dmat - m_t)
        inter = jnp.exp(g - m_t)
        num = _dot(s.astype(BF16), v) + inter * _dot(qb, cst.astype(BF16))
        den = jnp.sum(s, axis=-1, keepdims=True) + inter * jnp.sum(q * nst, axis=-1, keepdims=True)
        hval = num / jnp.maximum(jnp.abs(den), jnp.exp(-m_t))

        hn = _rms(hval, gh_ref[:, h * dv:(h + 1) * dv])
        og = o_ref[:, h * dv:(h + 1) * dv]
        a_ref[:, h * dv:(h + 1) * dv] = (hn * _sigmoid(og)).astype(BF16)

        b_end = bc[L - 1:L, :]
        m_new = m_t[L - 1:L, :]
        decay = jnp.exp(b_end + m_prev - m_new)
        wa = jnp.exp(b_end - bc + ic - m_new)
        kw = k * wa
        c_ref[h] = decay * cst + _dot_tn(kw.astype(BF16), v)
        n_ref[h] = decay * nst + jnp.sum(kw, axis=0, keepdims=True)
        m_ref[h] = jnp.broadcast_to(m_new, (1, LANES))


def _mlstm(z, gates_col, gates_row, bias_lanes, bias_rows, g_h, c0, n0, m0, chunk):
    b, t, _ = z.shape
    heads, dqk, dv = c0.shape[1:]
    qk = heads * dqk
    vd = heads * dv
    assert (2 * qk) % vd == 0
    v_blk = 2 * qk // vd
    ng = gates_row.shape[1]
    nc = t // chunk
    st4 = lambda i, c: (i, 0, 0, 0)
    return pl.pallas_call(
        functools.partial(_mlstm_body, heads=heads, dqk=dqk, dv=dv),
        out_shape=(
            jax.ShapeDtypeStruct((b, t, vd), BF16),
            jax.ShapeDtypeStruct((b, heads, dqk, dv), F32),
            jax.ShapeDtypeStruct((b, heads, 1, dqk), F32),
            jax.ShapeDtypeStruct((b, heads, 1, LANES), F32),
        ),
        grid=(b, nc),
        in_specs=[
            pl.BlockSpec((None, chunk, qk), lambda i, c: (i, c, 0)),
            pl.BlockSpec((None, chunk, qk), lambda i, c: (i, c, 1)),
            pl.BlockSpec((None, chunk, vd), lambda i, c: (i, c, v_blk)),
            pl.BlockSpec((None, chunk, vd), lambda i, c: (i, c, v_blk + 1)),
            pl.BlockSpec((None, chunk, LANES), lambda i, c: (i, c, 0)),
            pl.BlockSpec((None, ng, chunk), lambda i, c: (i, 0, c)),
            pl.BlockSpec((1, LANES), lambda i, c: (0, 0)),
            pl.BlockSpec((ng, 1), lambda i, c: (0, 0)),
            pl.BlockSpec((1, vd), lambda i, c: (0, 0)),
            pl.BlockSpec((None, heads, dqk, dv), st4),
            pl.BlockSpec((None, heads, 1, dqk), st4),
            pl.BlockSpec((None, heads, 1, LANES), st4),
        ],
        out_specs=(
            pl.BlockSpec((None, chunk, vd), lambda i, c: (i, c, 0)),
            pl.BlockSpec((None, heads, dqk, dv), st4),
            pl.BlockSpec((None, heads, 1, dqk), st4),
            pl.BlockSpec((None, heads, 1, LANES), st4),
        ),
        compiler_params=_params("parallel", "arbitrary"),
        name="mlstm",
    )(z, z, z, z, gates_col, gates_row, bias_lanes, bias_rows, g_h, c0, n0, m0)


def _cumsum_body(f_ref, b_ref, lf_ref, c_ref, *col_refs, blk, activate, anchor_end, heads):
    t = f_ref.shape[1]
    row = lax.broadcasted_iota(jnp.int32, (blk, blk), 0)
    col = lax.broadcasted_iota(jnp.int32, (blk, blk), 1)
    triu = (row <= col).astype(F32)
    carry = jnp.zeros((f_ref.shape[0], 1), F32)
    for i in range(t // blk):
        sl = slice(i * blk, (i + 1) * blk)
        seg = f_ref[:, sl]
        if activate:
            seg = _log_sigmoid(seg + b_ref[...])
        lf_ref[:, sl] = seg
        cs = _dot_f32(seg, triu) + carry
        c_ref[:, sl] = cs
        carry = cs[:, blk - 1:blk]
        if col_refs and not anchor_end:
            col_refs[0][sl, :] = _rows_to_cols(seg)
            col_refs[1][sl, :] = _rows_to_cols(cs)
    if anchor_end:
        c_ref[...] = c_ref[...] - carry
        if col_refs:
            sub = lax.broadcasted_iota(jnp.int32, (SUBLANES, blk), 0)
            ones3 = jnp.where(sub < 3, 1.0, 0.0)
            for i in range(t // blk):
                sl = slice(i * blk, (i + 1) * blk)
                c = c_ref[:heads, sl]
                col_refs[0][sl, :] = _rows_to_cols(jnp.concatenate([c, c, c, ones3], axis=0))


def _forget_cumsum(f_rows, bias_col, activate, anchor_end, emit_cols, heads=None):
    b, ng, t = f_rows.shape
    blk = _tile(t, 512)
    row_spec = pl.BlockSpec((None, ng, t), lambda i: (i, 0, 0))
    row_shape = jax.ShapeDtypeStruct((b, ng, t), F32)
    out_shape, out_specs = (row_shape, row_shape), (row_spec, row_spec)
    if emit_cols:
        col_shape = jax.ShapeDtypeStruct((b, t, LANES), F32)
        col_spec = pl.BlockSpec((None, t, LANES), lambda i: (i, 0, 0))
        n_cols = 1 if anchor_end else 2
        out_shape, out_specs = out_shape + (col_shape,) * n_cols, out_specs + (col_spec,) * n_cols
    return pl.pallas_call(
        functools.partial(_cumsum_body, blk=blk, activate=activate, anchor_end=anchor_end, heads=heads),
        out_shape=out_shape,
        grid=(b,),
        in_specs=[row_spec, pl.BlockSpec((ng, 1), lambda i: (0, 0))],
        out_specs=out_specs,
        compiler_params=_params("parallel"),
        name="forget_cumsum",
    )(f_rows, bias_col)


def _split3(c):
    hi = c.astype(BF16).astype(F32)
    r = c - hi
    mid = r.astype(BF16).astype(F32)
    return hi, mid, r - mid


def _bias_lanes(c, query_side):
    hi, mid, lo = _split3(c if query_side else -c)
    lane = lax.broadcasted_iota(jnp.int32, (c.shape[0], LANES), 1)
    o = 0 if query_side else 3
    terms = jnp.where(lane == o, hi, jnp.where(lane == o + 1, mid, jnp.where(lane == o + 2, lo, 0.0)))
    ones = jnp.logical_and(lane >= 3 - o, lane < 6 - o)
    return jnp.where(ones, 1.0, terms).astype(BF16)


def _online_softmax_step(s, v, m, l, acc):
    m_new = jnp.maximum(m, jnp.max(s, axis=-1, keepdims=True))
    p = jnp.exp2(s - m_new)
    alpha = jnp.exp2(m - m_new)
    l = alpha * l + jnp.sum(p, axis=-1, keepdims=True)
    acc = alpha * acc + _dot(p.astype(BF16), v)
    return m_new, l, acc


def _bias_rows(c, rows):
    hi, mid, lo = _split3(c)
    sub = lax.broadcasted_iota(jnp.int32, (rows, c.shape[1]), 0)
    terms = jnp.where(sub == 0, hi, jnp.where(sub == 1, mid, jnp.where(sub == 2, lo, 0.0)))
    return jnp.where(jnp.logical_and(sub >= 3, sub < 6), 1.0, terms).astype(BF16)


def _fox_prompt_body(q_ref, k_ref, v_ref, og_ref, cc_ref, cr_ref, a_ref, qt_ref, ka_ref, vt_ref, *, tq,
                     kbatch):
    h = pl.program_id(1)
    t, hd = q_ref.shape
    nt = t // tq
    for r in range(nt):
        rows = slice(r * tq, (r + 1) * tq)
        cc = cc_ref[rows, :]
        lane = lax.broadcasted_iota(jnp.int32, cc.shape, 1)
        c_col = jnp.sum(jnp.where(lane == h, cc, 0.0), axis=-1, keepdims=True) * LOG2E
        ka_ref[rows, :hd] = k_ref[rows, :]
        ka_ref[rows, hd:] = _bias_lanes(c_col, False)
        qt_ref[:hd, rows] = jnp.transpose(q_ref[rows, :].astype(F32)).astype(BF16)
        qt_ref[hd:, rows] = _bias_rows(cr_ref[r:r + 1, :] * LOG2E, LANES)
        vt_ref[:, rows] = jnp.transpose(v_ref[rows, :].astype(F32)).astype(BF16)

    key = lax.broadcasted_iota(jnp.int32, (tq, tq), 0)
    qry = lax.broadcasted_iota(jnp.int32, (tq, tq), 1)
    causal = key <= qry

    def tile_off(kj):
        return kj * tq if isinstance(kj, int) else pl.multiple_of(kj * tq, tq)

    for qi in range(nt):
        cols = slice(qi * tq, (qi + 1) * tq)
        qt = qt_ref[:, cols]

        def steps(carry, tiles):
            m, l, acc = carry
            scores = [_dot(ka_ref[pl.ds(tile_off(kj), tq), :], qt) for kj, _ in tiles]
            for s, (kj, diagonal) in zip(scores, tiles):
                if diagonal:
                    s = jnp.where(causal, s, NEG_INF)
                m_new = jnp.maximum(m, jnp.max(s, axis=0, keepdims=True))
                p = jnp.exp2(s - m_new)
                alpha = jnp.exp2(m - m_new)
                l = alpha * l + jnp.sum(p, axis=0, keepdims=True)
                acc = alpha * acc + _dot(vt_ref[:, pl.ds(tile_off(kj), tq)], p.astype(BF16))
                m = m_new
            return m, l, acc

        carry = (jnp.full((1, tq), NEG_INF, F32), jnp.zeros((1, tq), F32), jnp.zeros((hd, tq), F32))
        nb = qi // kbatch
        carry = lax.fori_loop(
            0, nb, lambda i, c: steps(c, [(i * kbatch + u, False) for u in range(kbatch)]), carry)
        tail = [(kj, False) for kj in range(nb * kbatch, qi)] + [(qi, True)]
        _, l, acc = steps(carry, tail)
        o = jnp.transpose(acc / l)
        a_ref[cols, :] = (o * _sigmoid(og_ref[cols, :])).astype(BF16)


def _fox_prompt_attn(qb, kb, vb, og, c_col, c_row, heads):
    b, t, d = qb.shape
    hd = d // heads
    tq = _tile(t, 512)
    nt = t // tq
    c_row4 = c_row.reshape(b, c_row.shape[1], nt, tq)
    head_blk = pl.BlockSpec((None, t, hd), lambda i, h: (i, 0, h))
    return pl.pallas_call(
        functools.partial(_fox_prompt_body, tq=tq, kbatch=ATTN_KEY_BATCH),
        out_shape=jax.ShapeDtypeStruct((b, t, d), BF16),
        grid=(b, heads),
        in_specs=[
            head_blk, head_blk, head_blk, head_blk,
            pl.BlockSpec((None, t, LANES), lambda i, h: (i, 0, 0)),
            pl.BlockSpec((None, None, nt, tq), lambda i, h: (i, h, 0, 0)),
        ],
        out_specs=head_blk,
        scratch_shapes=[pltpu.VMEM((hd + LANES, t), BF16), pltpu.VMEM((t, hd + LANES), BF16),
                        pltpu.VMEM((hd, t), BF16)],
        compiler_params=_params("parallel", "arbitrary"),
        name="fox_prompt_attn",
    )(qb, kb, vb, og, c_col, c_row4)


def _key_bias_lanes(c3, heads):
    lane = lax.broadcasted_iota(jnp.int32, c3.shape, 1)
    hi, mid, lo = _split3(c3 * jnp.where(lane < 3 * heads, LOG2E, 1.0))
    pick = jnp.where(lane < heads, hi, jnp.where(lane < 2 * heads, mid, jnp.where(lane < 3 * heads, lo, hi)))
    return pick.astype(BF16)


def _fox_sample_body(q_ref, kp_ref, vp_ref, kn_ref, vn_ref, og_ref, cq_ref, c3p_ref, c3n_ref,
                     a_ref, wq_ref, m_ref, l_ref, acc_ref, *, heads, hd, hg):
    j = pl.program_id(1)
    nj = pl.num_programs(1)
    tq = q_ref.shape[0]
    groups = heads // hg
    tk = kp_ref.shape[0] // heads

    @pl.when(j == 0)
    def _():
        m_ref[...] = jnp.full(m_ref.shape, NEG_INF, F32)
        l_ref[...] = jnp.zeros(l_ref.shape, F32)
        acc_ref[...] = jnp.zeros(acc_ref.shape, F32)
        wq_ref[...] = jnp.zeros(wq_ref.shape, BF16)
        lane = lax.broadcasted_iota(jnp.int32, (tq, LANES), 1)
        for g in range(groups):
            for i in range(hg):
                h = g * hg + i
                rows = slice(i * tq, (i + 1) * tq)
                wq_ref[g, rows, i * hd:(i + 1) * hd] = q_ref[:, h * hd:(h + 1) * hd]
                hi, mid, lo = _split3(cq_ref[:, h:h + 1] * LOG2E)
                own = jnp.logical_or(lane == h, jnp.logical_or(lane == heads + h, lane == 2 * heads + h))
                cq3 = jnp.where(lane == 3 * heads, hi,
                                jnp.where(lane == 3 * heads + 1, mid,
                                          jnp.where(lane == 3 * heads + 2, lo, 0.0)))
                wq_ref[g, rows, hg * hd:] = jnp.where(own, -1.0, cq3).astype(BF16)

    def update(g, s, v):
        m, l, acc = _online_softmax_step(s, v, m_ref[g], l_ref[g], acc_ref[g])
        m_ref[g] = m
        l_ref[g] = l
        acc_ref[g] = acc

    kbias = _key_bias_lanes(c3p_ref[...], heads)
    for g in range(groups):
        head_rows = lambda i: pl.ds(g * hg + i, tk, stride=heads)
        ks = [kp_ref[head_rows(i), :].astype(BF16) for i in range(hg)]
        vs = [vp_ref[head_rows(i), :].astype(BF16) for i in range(hg)]
        s = _dot_nt(wq_ref[g], jnp.concatenate(ks + [kbias], axis=1))
        update(g, s, jnp.concatenate(vs, axis=1))

    @pl.when(j == nj - 1)
    def _():
        row = lax.broadcasted_iota(jnp.int32, (hg * tq, tq), 0)
        col = lax.broadcasted_iota(jnp.int32, (hg * tq, tq), 1)
        causal = row % tq >= col
        nbias = _key_bias_lanes(c3n_ref[...], heads)
        for g in range(groups):
            cols = slice(g * hg * hd, (g + 1) * hg * hd)
            s = _dot_nt(wq_ref[g], jnp.concatenate([kn_ref[:, cols], nbias], axis=1))
            update(g, jnp.where(causal, s, NEG_INF), vn_ref[:, cols])
            for i in range(hg):
                rows = slice(i * tq, (i + 1) * tq)
                sl = slice((g * hg + i) * hd, (g * hg + i + 1) * hd)
                o = acc_ref[g, rows, i * hd:(i + 1) * hd] / l_ref[g, rows, :]
                a_ref[:, sl] = (o * _sigmoid(og_ref[:, sl])).astype(BF16)


def _fox_sample_attn(qb, k_past, v_past, layer, kb, vb, og, cq_col, c3_past, c3_new):
    b, tq, d = qb.shape
    p, heads, hd = k_past.shape[2:]
    assert 3 * heads + 3 <= LANES
    hg = max(1, min(heads, 512 // hd))
    assert heads % hg == 0
    tk = _tile(p, 1024)
    new_blk = pl.BlockSpec((None, tq, d), lambda i, j: (i, 0, 0))
    past_blk = pl.BlockSpec((None, None, tk * heads, hd), lambda i, j: (layer, i, j, 0))
    flat = lambda a: a.reshape(a.shape[:2] + (p * heads, hd))
    return pl.pallas_call(
        functools.partial(_fox_sample_body, heads=heads, hd=hd, hg=hg),
        out_shape=jax.ShapeDtypeStruct((b, tq, d), BF16),
        grid=(b, p // tk),
        in_specs=[
            new_blk, past_blk, past_blk, new_blk, new_blk, new_blk,
            pl.BlockSpec((None, tq, LANES), lambda i, j: (i, 0, 0)),
            pl.BlockSpec((None, tk, LANES), lambda i, j: (i, j, 0)),
            pl.BlockSpec((None, tq, LANES), lambda i, j: (i, 0, 0)),
        ],
        out_specs=new_blk,
        scratch_shapes=[
            pltpu.VMEM((heads // hg, hg * tq, hg * hd + LANES), BF16),
            pltpu.VMEM((heads // hg, hg * tq, 1), F32),
            pltpu.VMEM((heads // hg, hg * tq, 1), F32),
            pltpu.VMEM((heads // hg, hg * tq, hg * hd), F32),
        ],
        compiler_params=_params("parallel", "arbitrary"),
        name="fox_sample_attn",
    )(qb, flat(k_past), flat(v_past), kb, vb, og, cq_col, c3_past, c3_new)


def _pad_rows(a, rows):
    return jnp.pad(a, ((0, rows - a.shape[0]),) + ((0, 0),) * (a.ndim - 1))


def _col_layout(rows):
    c = jnp.swapaxes(rows, 1, 2)
    return jnp.pad(c, ((0, 0), (0, 0), (0, LANES - c.shape[-1])))


def _c3_layout(rows, heads):
    c = jnp.swapaxes(rows[:, :heads, :], 1, 2)
    c3 = jnp.concatenate([c, c, c, jnp.ones(c.shape[:2] + (3,), F32)], axis=-1)
    return jnp.pad(c3, ((0, 0), (0, 0), (0, LANES - c3.shape[-1])))


def _run_trunk(x, p, w, mlstm_init, fox_past):
    b, t, d = x.shape
    depth = w["norm_gains"].shape[0]
    m_heads, dqk, dv = w["m_heads"], w["dqk"], w["dv"]
    f_heads = w["f_heads"]
    hd = d // f_heads
    qk, vd = m_heads * dqk, m_heads * dv
    mtok = b * t
    xf = x.reshape(mtok, d)
    pf = p.reshape(depth, mtok, p.shape[-1])
    chunk = _tile(t, 256)
    m_states, f_rows, kv_stacks = [], [], None

    def ffn(xf, g, name, i):
        key = (name, i)
        if key in w["ffn_bf16"]:
            return _ffn(xf, g, w["ffn_bf16"][key], i)
        out, w["ffn_bf16"][key] = _ffn(xf, g, (w[name + "_in"], w[name + "_out"]), i)
        return out

    def in_proj_weight(name, j):
        return w["in_proj_bf16"].get((name, j), (w[name],))

    def keep_casts(name, j, casts):
        if casts:
            w["in_proj_bf16"][(name, j)] = casts

    for i in range(depth):
        gains = w["norm_gains"][i]
        gain = lambda r: gains[r].reshape(1, d)
        j = i // 2
        xf = ffn(xf, gain(0), "ffn1", i)
        if i % 2 == 0:
            (z, g_rows, g_cols), casts = _mlstm_in_proj(
                xf, gain(1), in_proj_weight("mlstm_w_in_t", j), j, 2 * qk + 2 * vd, 2 * m_heads)
            keep_casts("mlstm_w_in_t", j, casts)
            ng = g_rows.shape[0]
            g_rows = jnp.swapaxes(g_rows.reshape(ng, b, t), 0, 1)
            if mlstm_init is None:
                c0 = jnp.zeros((b, m_heads, dqk, dv), F32)
                n0 = jnp.zeros((b, m_heads, dqk), F32)
                m0 = jnp.zeros((b, m_heads), F32)
            else:
                c0, n0, m0 = mlstm_init[0][j], mlstm_init[1][j], mlstm_init[2][j]
            bias = w["mlstm_b_gates"][j].reshape(2 * m_heads)
            a, c_new, n_new, m_new = _mlstm(
                z.reshape(b, t, -1), g_cols.reshape(b, t, LANES), g_rows,
                jnp.pad(bias, (0, LANES - 2 * m_heads)).reshape(1, LANES),
                _pad_rows(bias.reshape(-1, 1), ng),
                w["mlstm_g_h"][j].reshape(1, vd),
                c0, n0.reshape(b, m_heads, 1, dqk),
                jnp.broadcast_to(m0[:, :, None, None], (b, m_heads, 1, LANES)), chunk)
            m_states.append((c_new, n_new.reshape(b, m_heads, dqk), m_new[:, :, 0, 0]))
            xf = _out_proj(a.reshape(mtok, vd), w["mlstm_w_out"], xf, j)
        else:
            (qb, k_stack, kb, v_stack, vb, og, f_pre), casts = _fox_in_proj(
                xf, gain(1), in_proj_weight("fox_w_in_t", j), w["fox_g_qk"][j].reshape(2, 1, hd), j,
                hd, hd ** -0.5 * LOG2E, kv_stacks)
            keep_casts("fox_w_in_t", j, casts)
            kv_stacks = (k_stack, v_stack)
            ng = f_pre.shape[0]
            f_pre = jnp.swapaxes(f_pre.reshape(ng, b, t), 0, 1)
            bias_col = _pad_rows(w["fox_b_f"][j].reshape(-1, 1), ng)
            shp = (b, t, d)
            if fox_past is None:
                lf_row, c_row, lf_col, c_col = _forget_cumsum(f_pre, bias_col, True, False, True)
                a = _fox_prompt_attn(qb.reshape(shp), kb.reshape(shp), vb.reshape(shp),
                                     og.reshape(shp), c_col, c_row, f_heads)
                lf = lf_col[:, :, :f_heads]
            else:
                lf_row, c_row = _forget_cumsum(f_pre, bias_col, True, False, False)
                lf_past_row = jnp.swapaxes(fox_past[2][j].astype(F32), 1, 2)
                lf_past_row = jnp.pad(lf_past_row, ((0, 0), (0, ng - f_heads), (0, 0)))
                _, _, c3_past = _forget_cumsum(lf_past_row, bias_col, False, True, True, f_heads)
                a = _fox_sample_attn(qb.reshape(shp), fox_past[0], fox_past[1], j,
                                     kb.reshape(shp), vb.reshape(shp), og.reshape(shp),
                                     _col_layout(c_row), c3_past, _c3_layout(c_row, f_heads))
                lf = jnp.swapaxes(lf_row[:, :f_heads, :], 1, 2)
            f_rows.append(lf)
            xf = _out_proj(a.reshape(mtok, d), w["fox_w_out"], xf, j)
        xf = ffn(xf, gain(2), "ffn2", i)
        xf = _ple(xf, gain(3), w["ple_gate"], pf, w["ple_proj"], i)
    stack = lambda k: jnp.stack([s[k] for s in m_states])
    k_all, v_all = (a.reshape(a.shape[0], b, t, f_heads, hd) for a in kv_stacks)
    return (xf.reshape(b, t, d), stack(0), stack(1), stack(2), k_all, v_all, jnp.stack(f_rows))


def kernel(x_prompt, x_sample, p_prompt, p_sample, state_mlstm_C, state_mlstm_n, state_mlstm_m, cache_fox_k, cache_fox_v, cache_fox_lf, norm_gains, ffn1_in, ffn1_out, ffn2_in, ffn2_out, ple_gate, ple_proj, mlstm_w_in, mlstm_b_gates, mlstm_g_h, mlstm_w_out, fox_w_in, fox_b_f, fox_g_qk, fox_w_out):
    m_heads, dqk, dv = state_mlstm_C.shape[2:]
    w = {
        "m_heads": m_heads, "dqk": dqk, "dv": dv, "f_heads": cache_fox_lf.shape[-1],
        "norm_gains": norm_gains, "mlstm_b_gates": mlstm_b_gates, "mlstm_g_h": mlstm_g_h,
        "fox_b_f": fox_b_f, "fox_g_qk": fox_g_qk,
        "ffn1_in": ffn1_in, "ffn1_out": ffn1_out, "ffn2_in": ffn2_in, "ffn2_out": ffn2_out,
        "ffn_bf16": {}, "in_proj_bf16": {},
        "ple_gate": ple_gate.astype(BF16), "ple_proj": ple_proj.astype(BF16),
        "mlstm_w_out": mlstm_w_out.astype(BF16), "fox_w_out": fox_w_out.astype(BF16),
        "mlstm_w_in_t": jnp.swapaxes(mlstm_w_in, 1, 2), "fox_w_in_t": jnp.swapaxes(fox_w_in, 1, 2),
    }
    y_s, s_c, s_n, s_m, s_k, s_v, s_lf = _run_trunk(
        x_sample, p_sample, w, (state_mlstm_C, state_mlstm_n, state_mlstm_m),
        (cache_fox_k, cache_fox_v, cache_fox_lf))
    y_p, p_c, p_n, p_m, p_k, p_v, p_lf = _run_trunk(x_prompt, p_prompt, w, None, None)
    return (y_p, y_s, p_c, p_n, p_m, p_k, p_v, p_lf, s_c, s_n, s_m, s_k, s_v, s_lf)
```
